```python
import jax, jax.numpy as jnp
from jax import lax
import numpy as np

D_MODEL = 1024
BATCH = 8
SEQ = 4096
DEPTH = 4

CTX_LEN = 256
GRID_W = 64
N_MIXERS = 3
N_HEADS = 16
HEAD_DIM = D_MODEL // N_HEADS
ATTN_SCALE = HEAD_DIM ** -0.5
WIN_ROWS_MAX = 8
WIN_COLS = 16
RPB_ROWS = 2 * WIN_ROWS_MAX - 1
RPB_COLS = 2 * WIN_COLS - 1
CONF_KERNEL = 31
SHORT_KERNEL = 3
D_FF = 4 * D_MODEL
N_MOD = 6
EPS = 1e-6
N_NA = (DEPTH + 2) // 3
N_CONF = (DEPTH + 1) // 3
N_SC = DEPTH // 3

kernel_name = "hybrid_na_conformer_shortconv_dit"


def rmsnorm(x, g):
    xf = x.astype(jnp.float32)
    y = xf * lax.rsqrt(jnp.mean(xf * xf, axis=-1, keepdims=True) + EPS)
    return (y * g.astype(jnp.float32)).astype(x.dtype)


def layernorm(x, g, b):
    xf = x.astype(jnp.float32)
    mu = jnp.mean(xf, axis=-1, keepdims=True)
    xc = xf - mu
    var = jnp.mean(xc * xc, axis=-1, keepdims=True)
    y = xc * lax.rsqrt(var + EPS) * g.astype(jnp.float32) + b.astype(jnp.float32)
    return y.astype(x.dtype)


def modulate(h, shift, scale):
    return h * (1 + scale) + shift


def dwconv(x, w):
    k = w.shape[0]
    return lax.conv_general_dilated(
        x, w[:, None, :].astype(x.dtype), window_strides=(1,),
        padding=[((k - 1) // 2, k // 2)],
        dimension_numbers=("NWC", "WIO", "NWC"),
        feature_group_count=x.shape[-1])


def split_heads(t, n):
    b, l, _ = t.shape
    return t.reshape(b, l, n, N_HEADS, HEAD_DIM).transpose(2, 0, 3, 1, 4)


def neighbourhood_attention(h_lat, h_ctx, wqkv, wo, rpb, ctx_out):
    b, l, d = h_lat.shape
    rows = l // GRID_W
    kh = min(WIN_ROWS_MAX, rows)
    q, k, v = split_heads(h_lat @ wqkv, 3)
    q = q * ATTN_SCALE
    if ctx_out:
        q_c, k_c, v_c = split_heads(h_ctx @ wqkv, 3)
    else:
        k_c, v_c = split_heads(h_ctx @ wqkv[:, D_MODEL:], 2)
    q_grid = q.reshape(b, N_HEADS, rows, GRID_W, HEAD_DIM)
    k_grid = k.reshape(b, N_HEADS, rows, GRID_W, HEAD_DIM)
    v_grid = v.reshape(b, N_HEADS, rows, GRID_W, HEAD_DIM)

    r_ar = np.arange(rows)
    row_start = np.clip(r_ar - kh // 2, 0, rows - kh)
    row_off = row_start[:, None] + np.arange(kh)[None, :] - r_ar[:, None] + WIN_ROWS_MAX - 1
    c_ar = np.arange(GRID_W)
    col_idx = np.clip(c_ar - WIN_COLS // 2, 0, GRID_W - WIN_COLS)[:, None] + np.arange(WIN_COLS)[None, :]
    col_off = col_idx - c_ar[:, None] + WIN_COLS - 1
    bias_all = rpb[:, row_off[:, :, None, None], col_off[None, None, :, :]]
    bias_all = bias_all.transpose(1, 0, 3, 2, 4)
    n_lat = kh * WIN_COLS

    def row_block(xs):
        q_r, rs, bias_r = xs
        k_rows = lax.dynamic_slice_in_dim(k_grid, rs, kh, axis=2)
        v_rows = lax.dynamic_slice_in_dim(v_grid, rs, kh, axis=2)
        k_nb = k_rows[:, :, :, col_idx, :]
        v_nb = v_rows[:, :, :, col_idx, :]
        s_lat = jnp.einsum("bhqd,bhaqkd->bhqak", q_r, k_nb).astype(jnp.float32)
        s_lat = s_lat + bias_r.astype(jnp.float32)[None]
        s_ctx = jnp.einsum("bhqd,bhcd->bhqc", q_r, k_c).astype(jnp.float32)
        s = jnp.concatenate([s_lat.reshape(b, N_HEADS, GRID_W, n_lat), s_ctx], axis=-1)
        p = jax.nn.softmax(s, axis=-1).astype(v_nb.dtype)
        p_lat = p[..., :n_lat].reshape(b, N_HEADS, GRID_W, kh, WIN_COLS)
        p_ctx = p[..., n_lat:]
        return (jnp.einsum("bhqak,bhaqkd->bhqd", p_lat, v_nb)
                + jnp.einsum("bhqc,bhcd->bhqd", p_ctx, v_c))

    o_grid = lax.map(row_block, (jnp.moveaxis(q_grid, 2, 0),
                                 jnp.asarray(row_start, dtype=jnp.int32),
                                 bias_all))
    o_lat = o_grid.transpose(1, 0, 3, 2, 4).reshape(b, l, d) @ wo

    o_ctx = None
    if ctx_out:
        s_c = jnp.einsum("bhqd,bhkd->bhqk", q_c * ATTN_SCALE, k_c).astype(jnp.float32)
        p_c = jax.nn.softmax(s_c, axis=-1).astype(v_c.dtype)
        o_c = jnp.einsum("bhqk,bhkd->bhqd", p_c, v_c)
        o_ctx = o_c.transpose(0, 2, 1, 3).reshape(b, h_ctx.shape[1], d) @ wo
    return o_lat, o_ctx


def conformer_conv(h, w1, b1, dw, dwb, ln_g, ln_b, w2, b2):
    u = h @ w1 + b1
    a, g = jnp.split(u, 2, axis=-1)
    u = a * jax.nn.sigmoid(g)
    u = dwconv(u, dw) + dwb
    u = jax.nn.silu(layernorm(u, ln_g, ln_b))
    return u @ w2 + b2


def short_gated_conv(h, w_in, conv_w, w_out):
    bg, cg, v = jnp.split(h @ w_in, 3, axis=-1)
    return (bg * dwconv(cg * v, conv_w)) @ w_out


def sq_relu_mlp(h, w1, w2):
    return jnp.square(jax.nn.relu(h @ w1)) @ w2


def setup_inputs(seed: int = 0) -> dict:
    key = jax.random.key(seed)
    ks = jax.random.split(key, 32)
    nrm = jax.random.normal
    f32 = jnp.float32
    d = D_MODEL
    return {
        "x": nrm(ks[0], (BATCH, SEQ, d), f32),
        "c": nrm(ks[1], (BATCH, d), f32),
        "ctx": nrm(ks[2], (BATCH, CTX_LEN, d), f32),
        "c_ctx": nrm(ks[3], (d,), f32),
        "mod_w": nrm(ks[4], (DEPTH, d, N_MOD * d), f32) * (0.5 * d ** -0.5),
        "mod_b": nrm(ks[5], (DEPTH, N_MOD * d), f32) * 0.01,
        "norm1_g": 1.0 + 0.01 * nrm(ks[6], (DEPTH, d), f32),
        "norm2_g": 1.0 + 0.01 * nrm(ks[7], (DEPTH, d), f32),
        "mlp_w1": nrm(ks[8], (DEPTH, d, D_FF), f32) * d ** -0.5,
        "mlp_w2": nrm(ks[9], (DEPTH, D_FF, d), f32) * D_FF ** -0.5,
        "na_wqkv": nrm(ks[10], (N_NA, d, 3 * d), f32) * d ** -0.5,
        "na_wo": nrm(ks[11], (N_NA, d, d), f32) * d ** -0.5,
        "na_rpb": nrm(ks[12], (N_NA, N_HEADS, RPB_ROWS, RPB_COLS), f32) * 0.1,
        "cv_w1": nrm(ks[13], (N_CONF, d, 2 * d), f32) * d ** -0.5,
        "cv_b1": nrm(ks[14], (N_CONF, 2 * d), f32) * 0.01,
        "cv_dw": nrm(ks[15], (N_CONF, CONF_KERNEL, d), f32) * CONF_KERNEL ** -0.5,
        "cv_dwb": nrm(ks[16], (N_CONF, d), f32) * 0.01,
        "cv_ln_g": 1.0 + 0.01 * nrm(ks[17], (N_CONF, d), f32),
        "cv_ln_b": nrm(ks[18], (N_CONF, d), f32) * 0.01,
        "cv_w2": nrm(ks[19], (N_CONF, d, d), f32) * d ** -0.5,
        "cv_b2": nrm(ks[20], (N_CONF, d), f32) * 0.01,
        "sc_win": nrm(ks[21], (N_SC, d, 3 * d), f32) * d ** -0.5,
        "sc_conv": nrm(ks[22], (N_SC, SHORT_KERNEL, d), f32) * SHORT_KERNEL ** -0.5,
        "sc_wout": nrm(ks[23], (N_SC, d, d), f32) * d ** -0.5,
        "final_g": 1.0 + 0.01 * nrm(ks[24], (d,), f32),
    }


def reference(x, c, ctx, c_ctx, mod_w, mod_b, norm1_g, norm2_g, mlp_w1, mlp_w2,
              na_wqkv, na_wo, na_rpb, cv_w1, cv_b1, cv_dw, cv_dwb, cv_ln_g, cv_ln_b,
              cv_w2, cv_b2, sc_win, sc_conv, sc_wout, final_g):
    b = x.shape[0]
    silu_c = jax.nn.silu(c)
    silu_cc = jax.nn.silu(c_ctx)[None]
    last_attn = max(i for i in range(DEPTH) if i % N_MIXERS == 0)
    for i in range(DEPTH):
        kind = i % N_MIXERS
        slot = i // N_MIXERS
        ctx_live = i < last_attn
        need_ctx_in = ctx_live or kind == 0
        mod_l = (silu_c @ mod_w[i] + mod_b[i]).reshape(b, N_MOD, 1, D_MODEL)
        a_l = modulate(rmsnorm(x, norm1_g[i]), mod_l[:, 0], mod_l[:, 1])
        a_c = None
        if need_ctx_in:
            mod_c = (silu_cc @ mod_w[i] + mod_b[i]).reshape(1, N_MOD, 1, D_MODEL)
            a_c = modulate(rmsnorm(ctx, norm1_g[i]), mod_c[:, 0], mod_c[:, 1])
        if kind == 0:
            y_l, y_c = neighbourhood_attention(a_l, a_c, na_wqkv[slot], na_wo[slot],
                                               na_rpb[slot], ctx_live)
        elif kind == 1:
            cp = (cv_w1[slot], cv_b1[slot], cv_dw[slot], cv_dwb[slot], cv_ln_g[slot],
                  cv_ln_b[slot], cv_w2[slot], cv_b2[slot])
            y_l = conformer_conv(a_l, *cp)
            y_c = conformer_conv(a_c, *cp) if ctx_live else None
        else:
            sp = (sc_win[slot], sc_conv[slot], sc_wout[slot])
            y_l = short_gated_conv(a_l, *sp)
            y_c = short_gated_conv(a_c, *sp) if ctx_live else None
        x = x + mod_l[:, 2] * y_l
        m_l = modulate(rmsnorm(x, norm2_g[i]), mod_l[:, 3], mod_l[:, 4])
        x = x + mod_l[:, 5] * sq_relu_mlp(m_l, mlp_w1[i], mlp_w2[i])
        if ctx_live:
            ctx = ctx + mod_c[:, 2] * y_c
            m_c = modulate(rmsnorm(ctx, norm2_g[i]), mod_c[:, 3], mod_c[:, 4])
            ctx = ctx + mod_c[:, 5] * sq_relu_mlp(m_c, mlp_w1[i], mlp_w2[i])
    return rmsnorm(x, final_g)
```

```python
import functools

import numpy as np
import jax
import jax.numpy as jnp
from jax import lax
from jax.experimental import pallas as pl
from jax.experimental.pallas import tpu as pltpu

N_HEADS = 16
GRID_W = 64
WIN_ROWS_MAX = 8
WIN_COLS = 16
N_MIXERS = 3
N_MOD = 6
EPS = 1e-6
MASK_BIAS = -1e30

LANES = 128
MOD_ROWS = 16
ATTN_Q_ROWS = 4
VMEM_LIMIT = 56 * 1024 * 1024

F32 = jnp.float32
BF16 = jnp.bfloat16


def _cparams(sem):
    return pltpu.CompilerParams(dimension_semantics=sem, vmem_limit_bytes=VMEM_LIMIT)


def _rmsnorm(x, g):
    return x * lax.rsqrt(jnp.mean(x * x, axis=-1, keepdims=True) + EPS) * g


def _dot(a, b):
    return jnp.dot(a, b, preferred_element_type=F32)


def _dot_t(a, b):
    return lax.dot_general(a, b, (((1,), (1,)), ((), ())), preferred_element_type=F32)


def _mod_kernel(cc_ref, w_ref, b_ref, o_ref):
    s = cc_ref[...]
    s = s * jax.nn.sigmoid(s)
    o_ref[...] = _dot(s.astype(BF16), w_ref[...].astype(BF16)) + b_ref[...]


def _modulation(cc, mod_w, mod_b):
    depth, d, n = mod_w.shape
    tn = n // 4
    out = pl.pallas_call(
        _mod_kernel,
        grid=(depth, n // tn),
        in_specs=[
            pl.BlockSpec((MOD_ROWS, d), lambda l, j: (0, 0)),
            pl.BlockSpec((None, d, tn), lambda l, j: (l, 0, j)),
            pl.BlockSpec((None, 1, tn), lambda l, j: (l, 0, j)),
        ],
        out_specs=pl.BlockSpec((None, MOD_ROWS, tn), lambda l, j: (l, 0, j)),
        out_shape=jax.ShapeDtypeStruct((depth, MOD_ROWS, n), F32),
        compiler_params=_cparams(("parallel", "parallel")),
        name="modulation",
    )(cc, mod_w, mod_b.reshape(depth, 1, n))
    return out.reshape(depth, MOD_ROWS, N_MOD, d)


def _norm_matmul_kernel(x_ref, mod_ref, g_ref, *rest, n_w, n_b, mod_off, epilogue):
    w_refs = rest[:n_w]
    b_refs = rest[n_w:n_w + n_b]
    out_refs = rest[n_w + n_b:-1]
    a_scr = rest[-1]
    j = pl.program_id(1)

    @pl.when(j == 0)
    def _():
        y = _rmsnorm(x_ref[...], g_ref[...])
        a = y * (1.0 + mod_ref[mod_off + 1:mod_off + 2, :]) + mod_ref[mod_off:mod_off + 1, :]
        a_scr[...] = a.astype(BF16)

    a = a_scr[...]
    accs = [_dot(a, w[...]) for w in w_refs]
    if n_b:
        accs = [acc + b[...] for acc, b in zip(accs, b_refs)]
    outs = epilogue(accs, j)
    for o_ref, o in zip(out_refs, outs):
        o_ref[...] = o.astype(o_ref.dtype)


def _norm_matmul(x, mod, gain, w, bias, *, layer, mod_off, n_rows, mod_row, tm, tn,
                 n_w, out_dtypes, epilogue, name):
    d = x.shape[1]
    n_out = w.shape[1] // n_w
    nj = n_out // tn
    in_specs = [
        pl.BlockSpec((tm, d), lambda i, j: (i, 0)),
        pl.BlockSpec((None, None, N_MOD, d), lambda i, j: (layer, mod_row(i), 0, 0)),
        pl.BlockSpec((None, 1, d), lambda i, j: (layer, 0, 0)),
    ]
    args = [x, mod, gain]
    for k in range(n_w):
        in_specs.append(pl.BlockSpec((d, tn), lambda i, j, k=k: (0, k * nj + j)))
        args.append(w)
    n_b = 0
    if bias is not None:
        n_b = n_w
        for k in range(n_w):
            in_specs.append(pl.BlockSpec((1, tn), lambda i, j, k=k: (0, k * nj + j)))
            args.append(bias)
    kern = functools.partial(_norm_matmul_kernel, n_w=n_w, n_b=n_b, mod_off=mod_off,
                             epilogue=epilogue)
    return pl.pallas_call(
        kern,
        grid=(n_rows // tm, nj),
        in_specs=in_specs,
        out_specs=[pl.BlockSpec((tm, tn), lambda i, j: (i, j)) for _ in out_dtypes],
        out_shape=[jax.ShapeDtypeStruct((n_rows, n_out), dt) for dt in out_dtypes],
        scratch_shapes=[pltpu.VMEM((tm, d), BF16)],
        compiler_params=_cparams(("parallel", "arbitrary")),
        name=name,
    )(*args)


def _attn_geometry(rows):
    kh = min(WIN_ROWS_MAX, rows)
    rq = ATTN_Q_ROWS
    n_blk = rows // rq
    kr = min(rows, rq + kh - 1)
    r_ar = np.arange(rows)
    row_start = np.clip(r_ar - kh // 2, 0, rows - kh)
    c_ar = np.arange(GRID_W)
    col_start = np.clip(c_ar - WIN_COLS // 2, 0, GRID_W - WIN_COLS)
    key_base = np.zeros(n_blk, np.int32)
    cls_of = np.zeros(n_blk, np.int32)
    patterns = []
    for blk in range(n_blk):
        r = np.arange(blk * rq, (blk + 1) * rq)
        kb = int(np.clip(row_start[r[0]], 0, rows - kr))
        assert row_start[r].min() >= kb and row_start[r].max() + kh <= kb + kr
        key_base[blk] = kb
        pat = (tuple(r - kb), tuple(row_start[r] - kb))
        if pat not in patterns:
            patterns.append(pat)
        cls_of[blk] = patterns.index(pat)
    qn, kn = rq * GRID_W, kr * GRID_W
    row_off = np.zeros((len(patterns), qn, kn), np.int32)
    col_off = np.zeros((len(patterns), qn, kn), np.int32)
    valid = np.zeros((len(patterns), qn, kn), bool)
    q_c = np.tile(c_ar, rq)
    k_r = np.repeat(np.arange(kr), GRID_W)
    k_c = np.tile(c_ar, kr)
    for ci, (r_rel, rs_rel) in enumerate(patterns):
        q_r = np.repeat(np.asarray(r_rel), GRID_W)
        q_rs = np.repeat(np.asarray(rs_rel), GRID_W)
        ro = k_r[None, :] - q_r[:, None] + WIN_ROWS_MAX - 1
        co = k_c[None, :] - q_c[:, None] + WIN_COLS - 1
        ok_r = (k_r[None, :] >= q_rs[:, None]) & (k_r[None, :] < q_rs[:, None] + kh)
        cs = col_start[q_c]
        ok_c = (k_c[None, :] >= cs[:, None]) & (k_c[None, :] < cs[:, None] + WIN_COLS)
        valid[ci] = ok_r & ok_c
        row_off[ci] = np.where(valid[ci], ro, 0)
        col_off[ci] = np.where(valid[ci], co, 0)
    return dict(rq=rq, kr=kr, n_blk=n_blk, key_base=key_base, cls_of=cls_of,
                row_off=row_off, col_off=col_off, valid=valid)


def _attn_bias_table(rpb, geo):
    b = rpb[:, geo["row_off"], geo["col_off"]]
    return jnp.where(geo["valid"][None], b, MASK_BIAS).astype(F32)


def _softmax_pv(s_parts, v_parts):
    m = functools.reduce(jnp.maximum, [jnp.max(s, axis=-1, keepdims=True) for s in s_parts])
    p_parts = [jnp.exp(s - m) for s in s_parts]
    l = functools.reduce(jnp.add, [jnp.sum(p, axis=-1, keepdims=True) for p in p_parts])
    o = functools.reduce(jnp.add, [_dot(p.astype(BF16), v) for p, v in zip(p_parts, v_parts)])
    return o / l


def _attn_lat_kernel(kb_ref, cls_ref, q_ref, k_ref, v_ref, kc_ref, vc_ref, bias_ref, o_ref,
                     *, qn, kn, n_blk):
    lane = lax.broadcasted_iota(jnp.int32, (1, LANES), 1)
    head_lo = lane < (LANES // 2)
    kc = kc_ref[...]
    vc = vc_ref[...]

    def body(blk, carry):
        q0 = pl.multiple_of(blk * qn, qn)
        k0 = pl.multiple_of(kb_ref[blk] * GRID_W, GRID_W)
        cls = cls_ref[blk]
        q = q_ref[pl.ds(q0, qn), :]
        kw = k_ref[pl.ds(k0, kn), :]
        vw = v_ref[pl.ds(k0, kn), :]
        outs = []
        for h, sel in enumerate((head_lo, jnp.logical_not(head_lo))):
            qm = jnp.where(sel, q, jnp.zeros_like(q))
            s_lat = _dot_t(qm, kw) + bias_ref[h, cls]
            s_ctx = _dot_t(qm, kc)
            outs.append(_softmax_pv([s_lat, s_ctx], [vw, vc]))
        o_ref[pl.ds(q0, qn), :] = jnp.where(head_lo, outs[0], outs[1]).astype(o_ref.dtype)
        return carry

    lax.fori_loop(0, n_blk, body, 0)


def _attention_latent(qkv, bias_tab, geo, *, batch, seq, ctx_len, d):
    n_hp = d // LANES
    t_lat = batch * seq
    qn, kn = geo["rq"] * GRID_W, geo["kr"] * GRID_W
    n_cls = bias_tab.shape[1]
    ctx_blk0 = t_lat // ctx_len
    kern = functools.partial(_attn_lat_kernel, qn=qn, kn=kn, n_blk=geo["n_blk"])
    grid_spec = pltpu.PrefetchScalarGridSpec(
        num_scalar_prefetch=2,
        grid=(n_hp, batch),
        in_specs=[
            pl.BlockSpec((seq, LANES), lambda hp, b, *_: (b, hp)),
            pl.BlockSpec((seq, LANES), lambda hp, b, *_: (b, n_hp + hp)),
            pl.BlockSpec((seq, LANES), lambda hp, b, *_: (b, 2 * n_hp + hp)),
            pl.BlockSpec((ctx_len, LANES), lambda hp, b, *_: (ctx_blk0 + b, n_hp + hp)),
            pl.BlockSpec((ctx_len, LANES), lambda hp, b, *_: (ctx_blk0 + b, 2 * n_hp + hp)),
            pl.BlockSpec((2, n_cls, qn, kn), lambda hp, b, *_: (hp, 0, 0, 0)),
        ],
        out_specs=pl.BlockSpec((seq, LANES), lambda hp, b, *_: (b, hp)),
    )
    return pl.pallas_call(
        kern,
        grid_spec=grid_spec,
        out_shape=jax.ShapeDtypeStruct((t_lat, d), BF16),
        compiler_params=_cparams(("parallel", "parallel")),
        name="attn_latent",
    )(jnp.asarray(geo["key_base"]), jnp.asarray(geo["cls_of"]),
      qkv, qkv, qkv, qkv, qkv, bias_tab)


def _attn_ctx_kernel(q_ref, k_ref, v_ref, o_ref):
    lane = lax.broadcasted_iota(jnp.int32, (1, LANES), 1)
    head_lo = lane < (LANES // 2)
    q = q_ref[...]
    k = k_ref[...]
    v = v_ref[...]
    outs = []
    for sel in (head_lo, jnp.logical_not(head_lo)):
        qm = jnp.where(sel, q, jnp.zeros_like(q))
        outs.append(_softmax_pv([_dot_t(qm, k)], [v]))
    o_ref[...] = jnp.where(head_lo, outs[0], outs[1]).astype(o_ref.dtype)


def _attention_ctx(qkv, *, batch, seq, ctx_len, d):
    n_hp = d // LANES
    blk0 = batch * seq // ctx_len
    return pl.pallas_call(
        _attn_ctx_kernel,
        grid=(batch, n_hp),
        in_specs=[
            pl.BlockSpec((ctx_len, LANES), lambda b, hp: (blk0 + b, hp)),
            pl.BlockSpec((ctx_len, LANES), lambda b, hp: (blk0 + b, n_hp + hp)),
            pl.BlockSpec((ctx_len, LANES), lambda b, hp: (blk0 + b, 2 * n_hp + hp)),
        ],
        out_specs=pl.BlockSpec((ctx_len, LANES), lambda b, hp: (b, hp)),
        out_shape=jax.ShapeDtypeStruct((batch * ctx_len, d), BF16),
        compiler_params=_cparams(("parallel", "parallel")),
        name="attn_ctx",
    )(qkv, qkv, qkv)


def _proj_residual_kernel(a_ref, w_ref, x_ref, mod_ref, o_ref):
    y = _dot(a_ref[...], w_ref[...])
    o_ref[...] = x_ref[...] + mod_ref[2:3, :] * y


def _proj_residual(a, w, x, mod, *, layer, n_rows, mod_row, tm):
    d = x.shape[1]
    return pl.pallas_call(
        _proj_residual_kernel,
        grid=(n_rows // tm,),
        in_specs=[
            pl.BlockSpec((tm, d), lambda i: (i, 0)),
            pl.BlockSpec((d, d), lambda i: (0, 0)),
            pl.BlockSpec((tm, d), lambda i: (i, 0)),
            pl.BlockSpec((None, None, N_MOD, d), lambda i: (layer, mod_row(i), 0, 0)),
        ],
        out_specs=pl.BlockSpec((tm, d), lambda i: (i, 0)),
        out_shape=jax.ShapeDtypeStruct((n_rows, d), F32),
        compiler_params=_cparams(("parallel",)),
        name="attn_out_residual",
    )(a, w, x, mod)


def _seq_edges(i, tm, t_lat, seq, ctx_len):
    t0 = i * tm
    seq_len = jnp.where(t0 < t_lat, seq, ctx_len)
    pos0 = lax.rem(t0, seq_len)
    return pos0 == 0, pos0 + tm == seq_len


def _fill_halo(scr, prev_ref, cur_ref, next_ref, first, last, halo, tm):
    prev = prev_ref[...]
    nxt = next_ref[...]
    scr[0:halo, :] = jnp.where(first, jnp.zeros_like(prev), prev)
    scr[halo:halo + tm, :] = cur_ref[...]
    scr[halo + tm:2 * halo + tm, :] = jnp.where(last, jnp.zeros_like(nxt), nxt)


CONF_HALO = 16
CONF_ROW_CHUNK = 64


def _conformer_out_kernel(up_ref, uc_ref, un_ref, x_ref, mod_ref, dw_ref, dwb_ref, lng_ref,
                          lnb_ref, w2_ref, b2_ref, o_ref, scr, cv, *, tm, taps, t_lat, seq,
                          ctx_len):
    d = x_ref.shape[1]
    n_lg = d // LANES
    first, last = _seq_edges(pl.program_id(0), tm, t_lat, seq, ctx_len)
    prev = up_ref[...]
    nxt = un_ref[...]
    prev = jnp.where(first, jnp.zeros_like(prev), prev)
    nxt = jnp.where(last, jnp.zeros_like(nxt), nxt)
    for c in range(n_lg):
        cols = slice(c * LANES, (c + 1) * LANES)
        scr[c, 0:CONF_HALO, :] = prev[:, cols]
        scr[c, CONF_HALO:CONF_HALO + tm, :] = uc_ref[:, cols]
        scr[c, CONF_HALO + tm:2 * CONF_HALO + tm, :] = nxt[:, cols]
    base = CONF_HALO - (taps - 1) // 2

    def lane_group(c, carry):
        w_c = dw_ref[c]
        b_c = dwb_ref[c]
        for r0 in range(0, tm, CONF_ROW_CHUNK):
            acc = jnp.broadcast_to(b_c, (CONF_ROW_CHUNK, LANES))
            for k in range(taps):
                acc = acc + w_c[k:k + 1, :] * scr[c, r0 + base + k:r0 + base + k + CONF_ROW_CHUNK, :]
            cv[c, r0:r0 + CONF_ROW_CHUNK, :] = acc
        return carry

    lax.fori_loop(0, n_lg, lane_group, 0)
    u = jnp.concatenate([cv[c] for c in range(n_lg)], axis=-1)
    mu = jnp.mean(u, axis=-1, keepdims=True)
    uc = u - mu
    var = jnp.mean(uc * uc, axis=-1, keepdims=True)
    y = uc * lax.rsqrt(var + EPS) * lng_ref[...] + lnb_ref[...]
    y = y * jax.nn.sigmoid(y)
    out = _dot(y.astype(BF16), w2_ref[...]) + b2_ref[...]
    o_ref[...] = x_ref[...] + mod_ref[2:3, :] * out


def _conformer_out(u, x, mod, dw, dwb, lng, lnb, w2, b2, *, layer, n_rows, mod_row, tm,
                   t_lat, seq, ctx_len):
    d = x.shape[1]
    taps = dw.shape[0]
    n_lg = d // LANES
    hb = tm // CONF_HALO
    n_hblk = u.shape[0] // CONF_HALO
    kern = functools.partial(_conformer_out_kernel, tm=tm, taps=taps, t_lat=t_lat, seq=seq,
                             ctx_len=ctx_len)
    row = lambda i: (0, 0)
    row3 = lambda i: (0, 0, 0)
    dw = dw.reshape(taps, n_lg, LANES).transpose(1, 0, 2)
    dwb = dwb.reshape(n_lg, 1, LANES)
    return pl.pallas_call(
        kern,
        grid=(n_rows // tm,),
        in_specs=[
            pl.BlockSpec((CONF_HALO, d), lambda i: (jnp.maximum(i * hb - 1, 0), 0)),
            pl.BlockSpec((tm, d), lambda i: (i, 0)),
            pl.BlockSpec((CONF_HALO, d), lambda i: (jnp.minimum((i + 1) * hb, n_hblk - 1), 0)),
            pl.BlockSpec((tm, d), lambda i: (i, 0)),
            pl.BlockSpec((None, None, N_MOD, d), lambda i: (layer, mod_row(i), 0, 0)),
            pl.BlockSpec((n_lg, taps, LANES), row3),
            pl.BlockSpec((n_lg, 1, LANES), row3),
            pl.BlockSpec((1, d), row),
            pl.BlockSpec((1, d), row),
            pl.BlockSpec((d, d), row),
            pl.BlockSpec((1, d), row),
        ],
        out_specs=pl.BlockSpec((tm, d), lambda i: (i, 0)),
        out_shape=jax.ShapeDtypeStruct((n_rows, d), F32),
        scratch_shapes=[pltpu.VMEM((n_lg, tm + 2 * CONF_HALO, LANES), F32),
                        pltpu.VMEM((n_lg, tm, LANES), F32)],
        compiler_params=_cparams(("parallel",)),
        name="conformer_out_residual",
    )(u, u, u, x, mod, dw, dwb, lng, lnb, w2, b2)


SC_HALO = 8


def _shortconv_out_kernel(gp_ref, gc_ref, gn_ref, bg_ref, x_ref, mod_ref, cw_ref, w_ref, o_ref,
                          scr, *, tm, t_lat, seq, ctx_len):
    first, last = _seq_edges(pl.program_id(0), tm, t_lat, seq, ctx_len)
    _fill_halo(scr, gp_ref, gc_ref, gn_ref, first, last, SC_HALO, tm)
    conv = (cw_ref[0:1, :] * scr[SC_HALO - 1:SC_HALO - 1 + tm, :]
            + cw_ref[1:2, :] * scr[SC_HALO:SC_HALO + tm, :]
            + cw_ref[2:3, :] * scr[SC_HALO + 1:SC_HALO + 1 + tm, :])
    a = bg_ref[...] * conv
    o_ref[...] = x_ref[...] + mod_ref[2:3, :] * _dot(a.astype(BF16), w_ref[...])


def _shortconv_out(bg, g, x, mod, cw, w, *, layer, n_rows, mod_row, tm, t_lat, seq, ctx_len):
    d = x.shape[1]
    hb = tm // SC_HALO
    n_hblk = g.shape[0] // SC_HALO
    kern = functools.partial(_shortconv_out_kernel, tm=tm, t_lat=t_lat, seq=seq, ctx_len=ctx_len)
    row = lambda i: (0, 0)
    return pl.pallas_call(
        kern,
        grid=(n_rows // tm,),
        in_specs=[
            pl.BlockSpec((SC_HALO, d), lambda i: (jnp.maximum(i * hb - 1, 0), 0)),
            pl.BlockSpec((tm, d), lambda i: (i, 0)),
            pl.BlockSpec((SC_HALO, d), lambda i: (jnp.minimum((i + 1) * hb, n_hblk - 1), 0)),
            pl.BlockSpec((tm, d), lambda i: (i, 0)),
            pl.BlockSpec((tm, d), lambda i: (i, 0)),
            pl.BlockSpec((None, None, N_MOD, d), lambda i: (layer, mod_row(i), 0, 0)),
            pl.BlockSpec((cw.shape[0], d), row),
            pl.BlockSpec((d, d), row),
        ],
        out_specs=pl.BlockSpec((tm, d), lambda i: (i, 0)),
        out_shape=jax.ShapeDtypeStruct((n_rows, d), F32),
        scratch_shapes=[pltpu.VMEM((tm + 2 * SC_HALO, d), F32)],
        compiler_params=_cparams(("parallel",)),
        name="shortconv_out_residual",
    )(g, g, g, bg, x, mod, cw, w)


def _mlp_down_kernel(h_ref, w_ref, x_ref, mod_ref, *rest, final):
    o_ref = rest[-1]
    x2 = x_ref[...] + mod_ref[5:6, :] * _dot(h_ref[...], w_ref[...])
    if final:
        x2 = _rmsnorm(x2, rest[0][...])
    o_ref[...] = x2


def _mlp_down(h, w, x, mod, final_g, *, layer, n_rows, mod_row, tm):
    d = x.shape[1]
    f = h.shape[1]
    in_specs = [
        pl.BlockSpec((tm, f), lambda i: (i, 0)),
        pl.BlockSpec((f, d), lambda i: (0, 0)),
        pl.BlockSpec((tm, d), lambda i: (i, 0)),
        pl.BlockSpec((None, None, N_MOD, d), lambda i: (layer, mod_row(i), 0, 0)),
    ]
    args = [h, w, x, mod]
    if final_g is not None:
        in_specs.append(pl.BlockSpec((1, d), lambda i: (0, 0)))
        args.append(final_g)
    return pl.pallas_call(
        functools.partial(_mlp_down_kernel, final=final_g is not None),
        grid=(n_rows // tm,),
        in_specs=in_specs,
        out_specs=pl.BlockSpec((tm, d), lambda i: (i, 0)),
        out_shape=jax.ShapeDtypeStruct((n_rows, d), F32),
        compiler_params=_cparams(("parallel",)),
        name="mlp_down_residual",
    )(*args)


def _ep_qkv(n_q_tiles, scale):
    def ep(accs, j):
        return [accs[0] * jnp.where(j < n_q_tiles, scale, 1.0)]
    return ep


def _ep_glu(accs, j):
    return [accs[0] * jax.nn.sigmoid(accs[1])]


def _ep_gated(accs, j):
    return [accs[0], accs[1] * accs[2]]


def _ep_relu2(accs, j):
    r = jnp.maximum(accs[0], 0.0)
    return [r * r]


def kernel(x, c, ctx, c_ctx, mod_w, mod_b, norm1_g, norm2_g, mlp_w1, mlp_w2, na_wqkv, na_wo,
           na_rpb, cv_w1, cv_b1, cv_dw, cv_dwb, cv_ln_g, cv_ln_b, cv_w2, cv_b2, sc_win, sc_conv,
           sc_wout, final_g):
    batch, seq, d = x.shape
    ctx_len = ctx.shape[1]
    depth = mod_w.shape[0]
    t_lat = batch * seq
    t_all = t_lat + batch * ctx_len
    head_dim = d // N_HEADS
    assert 2 * head_dim == LANES and batch < MOD_ROWS and seq % GRID_W == 0
    rows = seq // GRID_W
    assert rows % ATTN_Q_ROWS == 0
    tm = 512
    tm_conv = 256
    assert seq % tm == 0 and (batch * ctx_len) % tm == 0
    assert seq % tm_conv == 0 and ctx_len % tm_conv == 0
    last_attn = max(i for i in range(depth) if i % N_MIXERS == 0)

    def mod_row_fn(tile):
        def f(i):
            return jnp.where(i * tile < t_lat, (i * tile) // seq, batch)
        return f

    cc = jnp.concatenate([c, c_ctx[None], jnp.zeros((MOD_ROWS - batch - 1, d), F32)], axis=0)
    mod = _modulation(cc, mod_w, mod_b)

    tok = jnp.concatenate([x.reshape(t_lat, d), ctx.reshape(batch * ctx_len, d)], axis=0)
    g1 = norm1_g.reshape(depth, 1, d)
    g2 = norm2_g.reshape(depth, 1, d)
    geo = _attn_geometry(rows)

    for i in range(depth):
        kind = i % N_MIXERS
        slot = i // N_MIXERS
        ctx_live = i < last_attn
        n_rows = t_all if ctx_live else t_lat
        nm = functools.partial(_norm_matmul, layer=i, mod_row=mod_row_fn(tm), tm=tm)
        if kind == 0:
            wqkv = na_wqkv[slot].astype(BF16)
            (qkv,) = nm(tok, mod, g1, wqkv, None, mod_off=0, n_rows=t_all, tn=1024, n_w=1,
                        out_dtypes=[BF16], epilogue=_ep_qkv(d // 1024, head_dim ** -0.5),
                        name="norm_qkv")
            bias_tab = _attn_bias_table(na_rpb[slot], geo)
            o = _attention_latent(qkv, bias_tab, geo, batch=batch, seq=seq, ctx_len=ctx_len, d=d)
            if ctx_live:
                o_c = _attention_ctx(qkv, batch=batch, seq=seq, ctx_len=ctx_len, d=d)
                o = jnp.concatenate([o, o_c], axis=0)
            tok = _proj_residual(o, na_wo[slot].astype(BF16), tok, mod, layer=i, n_rows=n_rows,
                                 mod_row=mod_row_fn(tm), tm=tm)
        elif kind == 1:
            (u,) = nm(tok, mod, g1, cv_w1[slot].astype(BF16), cv_b1[slot][None], mod_off=0,
                      n_rows=n_rows, tn=512, n_w=2, out_dtypes=[F32], epilogue=_ep_glu,
                      name="norm_glu")
            tok = _conformer_out(u, tok, mod, cv_dw[slot], cv_dwb[slot][None],
                                 cv_ln_g[slot][None], cv_ln_b[slot][None],
                                 cv_w2[slot].astype(BF16), cv_b2[slot][None], layer=i,
                                 n_rows=n_rows, mod_row=mod_row_fn(tm_conv), tm=tm_conv,
                                 t_lat=t_lat, seq=seq, ctx_len=ctx_len)
        else:
            bg, g = nm(tok, mod, g1, sc_win[slot].astype(BF16), None, mod_off=0, n_rows=n_rows,
                       tn=512, n_w=3, out_dtypes=[F32, F32], epilogue=_ep_gated,
                       name="norm_gated")
            tok = _shortconv_out(bg, g, tok, mod, sc_conv[slot], sc_wout[slot].astype(BF16),
                                 layer=i, n_rows=n_rows, mod_row=mod_row_fn(tm_conv), tm=tm_conv,
                                 t_lat=t_lat, seq=seq, ctx_len=ctx_len)
        (h,) = nm(tok, mod, g2, mlp_w1[i].astype(BF16), None, mod_off=3, n_rows=n_rows, tn=1024,
                  n_w=1, out_dtypes=[BF16], epilogue=_ep_relu2, name="norm_mlp_up")
        fg = final_g[None] if i == depth - 1 else None
        tok = _mlp_down(h, mlp_w2[i].astype(BF16), tok, mod, fg, layer=i, n_rows=n_rows,
                        mod_row=mod_row_fn(tm), tm=tm)
    return tok[:t_lat].reshape(batch, seq, d)
```

```python
import functools

import numpy as np
import jax
import jax.numpy as jnp
from jax import lax
from jax.experimental import pallas as pl
from jax.experimental.pallas import tpu as pltpu

N_HEADS = 16
GRID_W = 64
WIN_ROWS_MAX = 8
WIN_COLS = 16
N_MIXERS = 3
N_MOD = 6
EPS = 1e-6
MASK_BIAS = -1e30
LOG2E = 1.4426950408889634

LANES = 128
MOD_ROWS = 16
ATTN_Q_ROWS = 4
ATTN_K_ROWS = 12
ROW_CHUNK = 256
VMEM_LIMIT = 56 * 1024 * 1024

F32 = jnp.float32
BF16 = jnp.bfloat16


def _cparams(sem):
    return pltpu.CompilerParams(dimension_semantics=sem, vmem_limit_bytes=VMEM_LIMIT)


def _resident(shape):
    zeros = (0,) * len(shape)
    return pl.BlockSpec(shape, lambda *_: zeros, pipeline_mode=pl.Buffered(1))


def _rmsnorm(x, g):
    return x * lax.rsqrt(jnp.mean(x * x, axis=-1, keepdims=True) + EPS) * g


def _modulate(y, mod_ref, off):
    return y * (1.0 + mod_ref[off + 1:off + 2, :]) + mod_ref[off:off + 1, :]


def _dot(a, b):
    return jnp.dot(a, b, preferred_element_type=F32)


def _dot_t(a, b):
    return lax.dot_general(a, b, (((1,), (1,)), ((), ())), preferred_element_type=F32)


def _mod_kernel(cc_ref, w_ref, b_ref, o_ref):
    s = cc_ref[...]
    s = s * jax.nn.sigmoid(s)
    o_ref[...] = _dot(s.astype(BF16), w_ref[...].astype(BF16)) + b_ref[...]


def _modulation(cc, mod_w, mod_b):
    depth, d, n = mod_w.shape
    tn = n // 4
    out = pl.pallas_call(
        _mod_kernel,
        grid=(depth, n // tn),
        in_specs=[
            pl.BlockSpec((MOD_ROWS, d), lambda l, j: (0, 0)),
            pl.BlockSpec((None, d, tn), lambda l, j: (l, 0, j)),
            pl.BlockSpec((None, 1, tn), lambda l, j: (l, 0, j)),
        ],
        out_specs=pl.BlockSpec((None, MOD_ROWS, tn), lambda l, j: (l, 0, j)),
        out_shape=jax.ShapeDtypeStruct((depth, MOD_ROWS, n), F32),
        compiler_params=_cparams(("parallel", "parallel")),
        name="modulation",
    )(cc, mod_w, mod_b.reshape(depth, 1, n))
    return out.reshape(depth, MOD_ROWS, N_MOD, d)


def _in_proj_kernel(x_ref, mod_ref, g_ref, w_ref, *rest, has_bias, write_out, tm):
    b_ref = rest[0] if has_bias else None
    out_refs = rest[1:] if has_bias else rest
    for c in range(tm // ROW_CHUNK):
        rows = slice(c * ROW_CHUNK, (c + 1) * ROW_CHUNK)
        a = _modulate(_rmsnorm(x_ref[rows, :], g_ref[...]), mod_ref, 0).astype(BF16)
        acc = _dot(a, w_ref[...])
        if has_bias:
            acc = acc + b_ref[...]
        write_out(acc, out_refs, rows)


def _in_proj(x, mod, gain, w, bias, *, layer, n_rows, mod_row, tm, out_widths, out_dtypes,
             write_out, name):
    d, n = w.shape
    in_specs = [
        pl.BlockSpec((tm, d), lambda i: (i, 0)),
        pl.BlockSpec((None, None, N_MOD, d), lambda i: (layer, mod_row(i), 0, 0)),
        pl.BlockSpec((None, 1, d), lambda i: (layer, 0, 0)),
        _resident((d, n)),
    ]
    args = [x, mod, gain, w]
    if bias is not None:
        in_specs.append(_resident((1, n)))
        args.append(bias)
    kern = functools.partial(_in_proj_kernel, has_bias=bias is not None, write_out=write_out,
                             tm=tm)
    return pl.pallas_call(
        kern,
        grid=(n_rows // tm,),
        in_specs=in_specs,
        out_specs=[pl.BlockSpec((tm, wd), lambda i: (i, 0)) for wd in out_widths],
        out_shape=[jax.ShapeDtypeStruct((n_rows, wd), dt)
                   for wd, dt in zip(out_widths, out_dtypes)],
        compiler_params=_cparams(("parallel",)),
        name=name,
    )(*args)


def _write_qkv(d, scale):
    def write(acc, out_refs, rows):
        (o_ref,) = out_refs
        o_ref[rows, :d] = (acc[:, :d] * scale).astype(o_ref.dtype)
        o_ref[rows, d:] = acc[:, d:].astype(o_ref.dtype)
    return write


def _write_glu(d):
    def write(acc, out_refs, rows):
        (o_ref,) = out_refs
        o_ref[rows, :] = acc[:, :d] * jax.nn.sigmoid(acc[:, d:])
    return write


def _write_gated(d):
    def write(acc, out_refs, rows):
        bg_ref, g_ref = out_refs
        bg_ref[rows, :] = acc[:, :d]
        g_ref[rows, :] = acc[:, d:2 * d] * acc[:, 2 * d:]
    return write


def _attn_geometry(rows):
    kh = min(WIN_ROWS_MAX, rows)
    rq, kr = ATTN_Q_ROWS, ATTN_K_ROWS
    assert rows % rq == 0 and rows >= kr and kr % rq == 0
    n_blk = rows // rq
    r_ar = np.arange(rows)
    row_start = np.clip(r_ar - kh // 2, 0, rows - kh)
    c_ar = np.arange(GRID_W)
    col_start = np.clip(c_ar - WIN_COLS // 2, 0, GRID_W - WIN_COLS)
    key_chunk = np.zeros(n_blk, np.int32)
    cls_of = np.zeros(n_blk, np.int32)
    patterns = []
    for blk in range(n_blk):
        r = np.arange(blk * rq, (blk + 1) * rq)
        kb = int(np.clip(row_start[r[0]] // rq * rq, 0, rows - kr))
        assert row_start[r].min() >= kb and row_start[r].max() + kh <= kb + kr
        key_chunk[blk] = kb // rq
        pat = (tuple(r - kb), tuple(row_start[r] - kb))
        if pat not in patterns:
            patterns.append(pat)
        cls_of[blk] = patterns.index(pat)
    n_rpb_rows = 2 * WIN_ROWS_MAX - 1
    row_idx = np.full((len(patterns), kr, rq), n_rpb_rows, np.int32)
    for ci, (r_rel, rs_rel) in enumerate(patterns):
        for qi in range(rq):
            for ki in range(kr):
                if rs_rel[qi] <= ki < rs_rel[qi] + kh:
                    row_idx[ci, ki, qi] = ki - r_rel[qi] + WIN_ROWS_MAX - 1
    col_ok = ((c_ar[:, None] >= col_start[None, :])
              & (c_ar[:, None] < col_start[None, :] + WIN_COLS))
    col_off = c_ar[:, None] - c_ar[None, :] + WIN_COLS - 1
    onehot = np.zeros((2 * WIN_COLS - 1, GRID_W, GRID_W), np.float32)
    kc_i, c_i = np.nonzero(col_ok)
    onehot[col_off[kc_i, c_i], kc_i, c_i] = 1.0
    return dict(rq=rq, kr=kr, n_blk=n_blk, key_chunk=key_chunk, cls_of=cls_of,
                row_idx=row_idx, col_ok=col_ok, onehot=onehot)


def _attn_bias_table(rpb, geo):
    h = rpb.shape[0]
    rq, kr = geo["rq"], geo["kr"]
    toe = jnp.einsum("hrj,jkc->hrkc", rpb, jnp.asarray(geo["onehot"]),
                     precision=lax.Precision.HIGHEST)
    toe = jnp.where(geo["col_ok"][None, None], toe * LOG2E, MASK_BIAS)
    ext = jnp.concatenate([toe, jnp.full((h, 1, GRID_W, GRID_W), MASK_BIAS, F32)], axis=1)
    idx = geo["row_idx"]
    blocks = jnp.stack([ext[:, int(i)] for i in idx.ravel()], axis=1)
    n_cls = idx.shape[0]
    blocks = blocks.reshape(h, n_cls, kr, rq, GRID_W, GRID_W).transpose(0, 1, 2, 4, 3, 5)
    return blocks.reshape(h, n_cls, kr * GRID_W, rq * GRID_W)


def _attn_lat_kernel(kch_ref, cls_ref, q_ref, k_ref, v_ref, kc_ref, vc_ref, bias_ref, o_ref,
                     vt_scr, kcat_scr, s_scr, m_scr, *, qn, kn, n_blk):
    seq = q_ref.shape[0]
    n_vch = seq // qn
    n_kch = kn // qn
    lane = lax.broadcasted_iota(jnp.int32, (1, LANES), 1)
    sub = lax.broadcasted_iota(jnp.int32, (LANES, 1), 0)
    lane_lo = lane < (LANES // 2)
    sub_lo = sub < (LANES // 2)
    for j in range(n_vch):
        vt_scr[j] = v_ref[j * qn:(j + 1) * qn, :].astype(F32).T.astype(BF16)
    vt_scr[n_vch] = vc_ref[...].astype(F32).T.astype(BF16)
    kcat_scr[kn:, :] = kc_ref[...]

    def stage_a(blk, slot):
        q0 = pl.multiple_of(blk * qn, qn)
        k0 = pl.multiple_of(kch_ref[blk] * qn, qn)
        cls = cls_ref[blk]
        kcat_scr[0:kn, :] = k_ref[pl.ds(k0, kn), :]
        q = q_ref[pl.ds(q0, qn), :]
        kcat = kcat_scr[...]
        for h, sel in enumerate((lane_lo, jnp.logical_not(lane_lo))):
            qm = jnp.where(sel, q, jnp.zeros_like(q))
            s = _dot_t(kcat, qm)
            s_lat = s[:kn, :] + bias_ref[h, cls]
            s_ctx = s[kn:, :]
            s_scr[slot, h, 0:kn, :] = s_lat
            s_scr[slot, h, kn:, :] = s_ctx
            m = jnp.maximum(jnp.max(s_lat, axis=0, keepdims=True),
                            jnp.max(s_ctx, axis=0, keepdims=True))
            m_scr[slot, h] = jnp.broadcast_to(m, (8, qn))

    def stage_b(blk, slot):
        q0 = pl.multiple_of(blk * qn, qn)
        kch = kch_ref[blk]
        vt = jnp.concatenate([vt_scr[kch + j] for j in range(n_kch)] + [vt_scr[n_vch]], axis=1)
        outs = []
        for h in range(2):
            m = m_scr[slot, h][0:1, :]
            p = jnp.exp2(s_scr[slot, h] - m)
            l = jnp.sum(p, axis=0, keepdims=True)
            o_t = _dot(vt, p.astype(BF16))
            outs.append(o_t * (1.0 / l))
        o_t = jnp.where(sub_lo, outs[0], outs[1])
        o_ref[pl.ds(q0, qn), :] = o_t.T.astype(o_ref.dtype)

    stage_a(0, 0)

    def body(blk, carry):
        slot = lax.rem(blk, 2)
        stage_a(blk, slot)
        stage_b(blk - 1, 1 - slot)
        return carry

    lax.fori_loop(1, n_blk, body, 0)
    stage_b(n_blk - 1, (n_blk - 1) % 2)


def _attention_latent(qkv, bias_tab, geo, *, batch, seq, ctx_len, d, n_out_rows):
    n_hp = d // LANES
    t_lat = batch * seq
    qn, kn = geo["rq"] * GRID_W, geo["kr"] * GRID_W
    n_cls = bias_tab.shape[1]
    ctx_blk0 = t_lat // ctx_len
    kern = functools.partial(_attn_lat_kernel, qn=qn, kn=kn, n_blk=geo["n_blk"])
    grid_spec = pltpu.PrefetchScalarGridSpec(
        num_scalar_prefetch=2,
        grid=(n_hp, batch),
        in_specs=[
            pl.BlockSpec((seq, LANES), lambda hp, b, *_: (b, hp)),
            pl.BlockSpec((seq, LANES), lambda hp, b, *_: (b, n_hp + hp)),
            pl.BlockSpec((seq, LANES), lambda hp, b, *_: (b, 2 * n_hp + hp)),
            pl.BlockSpec((ctx_len, LANES), lambda hp, b, *_: (ctx_blk0 + b, n_hp + hp)),
            pl.BlockSpec((ctx_len, LANES), lambda hp, b, *_: (ctx_blk0 + b, 2 * n_hp + hp)),
            pl.BlockSpec((2, n_cls, kn, qn), lambda hp, b, *_: (hp, 0, 0, 0)),
        ],
        out_specs=pl.BlockSpec((seq, LANES), lambda hp, b, *_: (b, hp)),
        scratch_shapes=[pltpu.VMEM((seq // qn + 1, LANES, qn), BF16),
                        pltpu.VMEM((kn + ctx_len, LANES), BF16),
                        pltpu.VMEM((2, 2, kn + ctx_len, qn), F32),
                        pltpu.VMEM((2, 2, 8, qn), F32)],
    )
    return pl.pallas_call(
        kern,
        grid_spec=grid_spec,
        out_shape=jax.ShapeDtypeStruct((n_out_rows, d), BF16),
        compiler_params=_cparams(("parallel", "parallel")),
        name="attn_latent",
    )(jnp.asarray(geo["key_chunk"]), jnp.asarray(geo["cls_of"]),
      qkv, qkv, qkv, qkv, qkv, bias_tab)


def _attn_ctx_kernel(q_ref, k_ref, v_ref, o_in_ref, o_ref):
    del o_in_ref
    lane = lax.broadcasted_iota(jnp.int32, (1, LANES), 1)
    lane_lo = lane < (LANES // 2)
    q = q_ref[...]
    k = k_ref[...]
    v = v_ref[...]
    outs = []
    for sel in (lane_lo, jnp.logical_not(lane_lo)):
        qm = jnp.where(sel, q, jnp.zeros_like(q))
        s = _dot_t(qm, k)
        p = jnp.exp2(s - jnp.max(s, axis=-1, keepdims=True))
        l = jnp.sum(p, axis=-1, keepdims=True)
        outs.append(_dot(p.astype(BF16), v) / l)
    o_ref[...] = jnp.where(lane_lo, outs[0], outs[1]).astype(o_ref.dtype)


def _attention_ctx(qkv, o_all, *, batch, seq, ctx_len, d):
    n_hp = d // LANES
    blk0 = batch * seq // ctx_len
    return pl.pallas_call(
        _attn_ctx_kernel,
        grid=(batch, n_hp),
        in_specs=[
            pl.BlockSpec((ctx_len, LANES), lambda b, hp: (blk0 + b, hp)),
            pl.BlockSpec((ctx_len, LANES), lambda b, hp: (blk0 + b, n_hp + hp)),
            pl.BlockSpec((ctx_len, LANES), lambda b, hp: (blk0 + b, 2 * n_hp + hp)),
            pl.BlockSpec(memory_space=pl.ANY),
        ],
        out_specs=pl.BlockSpec((ctx_len, LANES), lambda b, hp: (blk0 + b, hp)),
        out_shape=jax.ShapeDtypeStruct(o_all.shape, o_all.dtype),
        input_output_aliases={3: 0},
        compiler_params=_cparams(("parallel", "parallel")),
        name="attn_ctx",
    )(qkv, qkv, qkv, o_all)


def _seq_edges(i, tm, t_lat, seq, ctx_len):
    t0 = i * tm
    seq_len = jnp.where(t0 < t_lat, seq, ctx_len)
    pos0 = lax.rem(t0, seq_len)
    return pos0 == 0, pos0 + tm == seq_len


CONF_HALO = 16
CONF_ROW_CHUNK = 64
SC_HALO = 8


def _conformer_core(up_ref, uc_ref, un_ref, dw_ref, dwb_ref, lng_ref, lnb_ref, scr, cv, *,
                    tm, first, last):
    n_lg, taps = dw_ref.shape[0], dw_ref.shape[1]
    prev = up_ref[...]
    nxt = un_ref[...]
    prev = jnp.where(first, jnp.zeros_like(prev), prev)
    nxt = jnp.where(last, jnp.zeros_like(nxt), nxt)
    for c in range(n_lg):
        cols = slice(c * LANES, (c + 1) * LANES)
        scr[c, 0:CONF_HALO, :] = prev[:, cols]
        scr[c, CONF_HALO:CONF_HALO + tm, :] = uc_ref[:, cols]
        scr[c, CONF_HALO + tm:2 * CONF_HALO + tm, :] = nxt[:, cols]
    base = CONF_HALO - (taps - 1) // 2

    def lane_group(c, carry):
        w_c = dw_ref[c]
        b_c = dwb_ref[c]
        for r0 in range(0, tm, CONF_ROW_CHUNK):
            acc = jnp.broadcast_to(b_c, (CONF_ROW_CHUNK, LANES))
            for k in range(taps):
                acc = acc + w_c[k:k + 1, :] * scr[c, r0 + base + k:r0 + base + k + CONF_ROW_CHUNK, :]
            cv[c, r0:r0 + CONF_ROW_CHUNK, :] = acc
        return carry

    lax.fori_loop(0, n_lg, lane_group, 0)
    u = jnp.concatenate([cv[c] for c in range(n_lg)], axis=-1)
    mu = jnp.mean(u, axis=-1, keepdims=True)
    uc = u - mu
    var = jnp.mean(uc * uc, axis=-1, keepdims=True)
    y = uc * lax.rsqrt(var + EPS) * lng_ref[...] + lnb_ref[...]
    return (y * jax.nn.sigmoid(y)).astype(BF16)


def _shortconv_core(gp_ref, gc_ref, gn_ref, bg_ref, cw_ref, scr, *, tm, first, last):
    prev = gp_ref[...]
    nxt = gn_ref[...]
    scr[0:SC_HALO, :] = jnp.where(first, jnp.zeros_like(prev), prev)
    scr[SC_HALO:SC_HALO + tm, :] = gc_ref[...]
    scr[SC_HALO + tm:2 * SC_HALO + tm, :] = jnp.where(last, jnp.zeros_like(nxt), nxt)
    conv = (cw_ref[0:1, :] * scr[SC_HALO - 1:SC_HALO - 1 + tm, :]
            + cw_ref[1:2, :] * scr[SC_HALO:SC_HALO + tm, :]
            + cw_ref[2:3, :] * scr[SC_HALO + 1:SC_HALO + 1 + tm, :])
    return (bg_ref[...] * conv).astype(BF16)


def _post_mixer_kernel(*refs, variant, n_in, n_scr, has_bias, final, tm, t_lat, seq, ctx_len):
    mixer_refs = refs[:n_in]
    x_ref, mod_ref, g2_ref, wout_ref, w1_ref, w2_ref = refs[n_in:n_in + 6]
    rest = list(refs[n_in + 6:])
    bout_ref = rest.pop(0) if has_bias else None
    fg_ref = rest.pop(0) if final else None
    o_ref = rest.pop(0)
    scr_refs = rest
    assert len(scr_refs) == n_scr

    def tail(a, rows):
        y = _dot(a, wout_ref[...])
        if has_bias:
            y = y + bout_ref[...]
        x1 = x_ref[rows, :] + mod_ref[2:3, :] * y
        m = _modulate(_rmsnorm(x1, g2_ref[...]), mod_ref, 3).astype(BF16)
        h = jnp.maximum(_dot(m, w1_ref[...]), 0.0)
        h = (h * h).astype(BF16)
        x2 = x1 + mod_ref[5:6, :] * _dot(h, w2_ref[...])
        if final:
            x2 = _rmsnorm(x2, fg_ref[...])
        o_ref[rows, :] = x2

    if variant == "plain":
        (a_ref,) = mixer_refs
        for c in range(tm // ROW_CHUNK):
            rows = slice(c * ROW_CHUNK, (c + 1) * ROW_CHUNK)
            tail(a_ref[rows, :], rows)
    else:
        first, last = _seq_edges(pl.program_id(0), tm, t_lat, seq, ctx_len)
        core = _conformer_core if variant == "conformer" else _shortconv_core
        a = core(*mixer_refs, *scr_refs, tm=tm, first=first, last=last)
        tail(a, slice(0, tm))


def _post_mixer(variant, mixer_args, mixer_specs, scratch, x, mod, g2, wout, bout, w1, w2,
                final_g, *, layer, n_rows, mod_row, tm, t_lat, seq, ctx_len):
    d = x.shape[1]
    f = w1.shape[1]
    in_specs = list(mixer_specs) + [
        pl.BlockSpec((tm, d), lambda i: (i, 0)),
        pl.BlockSpec((None, None, N_MOD, d), lambda i: (layer, mod_row(i), 0, 0)),
        pl.BlockSpec((None, 1, d), lambda i: (layer, 0, 0)),
        _resident((d, d)),
        _resident((d, f)),
        _resident((f, d)),
    ]
    args = list(mixer_args) + [x, mod, g2, wout, w1, w2]
    if bout is not None:
        in_specs.append(_resident((1, d)))
        args.append(bout)
    if final_g is not None:
        in_specs.append(_resident((1, d)))
        args.append(final_g)
    kern = functools.partial(_post_mixer_kernel, variant=variant, n_in=len(mixer_args),
                             n_scr=len(scratch), has_bias=bout is not None,
                             final=final_g is not None, tm=tm, t_lat=t_lat, seq=seq,
                             ctx_len=ctx_len)
    return pl.pallas_call(
        kern,
        grid=(n_rows // tm,),
        in_specs=in_specs,
        out_specs=pl.BlockSpec((tm, d), lambda i: (i, 0)),
        out_shape=jax.ShapeDtypeStruct((n_rows, d), F32),
        scratch_shapes=list(scratch),
        compiler_params=_cparams(("parallel",)),
        name="post_mixer_" + variant,
    )(*args)


def _halo_specs(n_total_rows, tm, halo, d):
    hb = tm // halo
    n_hblk = n_total_rows // halo
    return [
        pl.BlockSpec((halo, d), lambda i: (jnp.maximum(i * hb - 1, 0), 0)),
        pl.BlockSpec((tm, d), lambda i: (i, 0)),
        pl.BlockSpec((halo, d), lambda i: (jnp.minimum((i + 1) * hb, n_hblk - 1), 0)),
    ]


def kernel(x, c, ctx, c_ctx, mod_w, mod_b, norm1_g, norm2_g, mlp_w1, mlp_w2, na_wqkv, na_wo,
           na_rpb, cv_w1, cv_b1, cv_dw, cv_dwb, cv_ln_g, cv_ln_b, cv_w2, cv_b2, sc_win, sc_conv,
           sc_wout, final_g):
    batch, seq, d = x.shape
    ctx_len = ctx.shape[1]
    depth = mod_w.shape[0]
    t_lat = batch * seq
    t_all = t_lat + batch * ctx_len
    head_dim = d // N_HEADS
    assert 2 * head_dim == LANES and batch < MOD_ROWS and seq % GRID_W == 0
    rows = seq // GRID_W
    tm = 2 * ROW_CHUNK
    tm_conv = ROW_CHUNK
    assert seq % tm == 0 and (batch * ctx_len) % tm == 0
    assert seq % tm_conv == 0 and ctx_len % tm_conv == 0
    last_attn = max(i for i in range(depth) if i % N_MIXERS == 0)
    n_lg = d // LANES

    def mod_row_fn(tile):
        def f(i):
            return jnp.where(i * tile < t_lat, (i * tile) // seq, batch)
        return f

    cc = jnp.concatenate([c, c_ctx[None], jnp.zeros((MOD_ROWS - batch - 1, d), F32)], axis=0)
    mod = _modulation(cc, mod_w, mod_b)

    tok = jnp.concatenate([x.reshape(t_lat, d), ctx.reshape(batch * ctx_len, d)], axis=0)
    g1 = norm1_g.reshape(depth, 1, d)
    g2 = norm2_g.reshape(depth, 1, d)
    geo = _attn_geometry(rows)

    for i in range(depth):
        kind = i % N_MIXERS
        slot = i // N_MIXERS
        ctx_live = i < last_attn
        n_rows = t_all if ctx_live else t_lat
        inp = functools.partial(_in_proj, tok, mod, g1, layer=i, mod_row=mod_row_fn(tm), tm=tm)
        post = functools.partial(
            _post_mixer, x=tok, mod=mod, g2=g2, w1=mlp_w1[i].astype(BF16),
            w2=mlp_w2[i].astype(BF16), final_g=final_g[None] if i == depth - 1 else None,
            layer=i, n_rows=n_rows, t_lat=t_lat, seq=seq, ctx_len=ctx_len)
        if kind == 0:
            (qkv,) = inp(na_wqkv[slot].astype(BF16), None, n_rows=t_all, out_widths=[3 * d],
                         out_dtypes=[BF16], write_out=_write_qkv(d, head_dim ** -0.5 * LOG2E),
                         name="in_proj_qkv")
            bias_tab = _attn_bias_table(na_rpb[slot], geo)
            o = _attention_latent(qkv, bias_tab, geo, batch=batch, seq=seq, ctx_len=ctx_len,
                                  d=d, n_out_rows=n_rows)
            if ctx_live:
                o = _attention_ctx(qkv, o, batch=batch, seq=seq, ctx_len=ctx_len, d=d)
            tok = post("plain", [o], [pl.BlockSpec((tm, d), lambda r: (r, 0))], [],
                       wout=na_wo[slot].astype(BF16), bout=None, mod_row=mod_row_fn(tm), tm=tm)
        elif kind == 1:
            (u,) = inp(cv_w1[slot].astype(BF16), cv_b1[slot][None], n_rows=n_rows,
                       out_widths=[d], out_dtypes=[F32], write_out=_write_glu(d),
                       name="in_proj_glu")
            taps = cv_dw.shape[1]
            dw = cv_dw[slot].reshape(taps, n_lg, LANES).transpose(1, 0, 2)
            args = [u, u, u, dw, cv_dwb[slot].reshape(n_lg, 1, LANES), cv_ln_g[slot][None],
                    cv_ln_b[slot][None]]
            specs = _halo_specs(n_rows, tm_conv, CONF_HALO, d) + [
                _resident((n_lg, taps, LANES)), _resident((n_lg, 1, LANES)),
                _resident((1, d)), _resident((1, d))]
            scratch = [pltpu.VMEM((n_lg, tm_conv + 2 * CONF_HALO, LANES), F32),
                       pltpu.VMEM((n_lg, tm_conv, LANES), F32)]
            tok = post("conformer", args, specs, scratch, wout=cv_w2[slot].astype(BF16),
                       bout=cv_b2[slot][None], mod_row=mod_row_fn(tm_conv), tm=tm_conv)
        else:
            bg, g = inp(sc_win[slot].astype(BF16), None, n_rows=n_rows, out_widths=[d, d],
                        out_dtypes=[F32, F32], write_out=_write_gated(d), name="in_proj_gated")
            args = [g, g, g, bg, sc_conv[slot]]
            specs = _halo_specs(n_rows, tm_conv, SC_HALO, d) + [
                pl.BlockSpec((tm_conv, d), lambda r: (r, 0)), _resident(sc_conv[slot].shape)]
            scratch = [pltpu.VMEM((tm_conv + 2 * SC_HALO, d), F32)]
            tok = post("shortconv", args, specs, scratch, wout=sc_wout[slot].astype(BF16),
                       bout=None, mod_row=mod_row_fn(tm_conv), tm=tm_conv)
    return tok[:t_lat].reshape(batch, seq, d)
```

```python
import functools

import numpy as np
import jax
import jax.numpy as jnp
from jax import lax
from jax.experimental import pallas as pl
from jax.experimental.pallas import tpu as pltpu

N_HEADS = 16
GRID_W = 64
WIN_ROWS_MAX = 8
WIN_COLS = 16
N_MIXERS = 3
N_MOD = 6
EPS = 1e-6
MASK_BIAS = -1e30
LOG2E = 1.4426950408889634

LANES = 128
MOD_ROWS = 16
ATTN_Q_ROWS = 4
ATTN_K_ROWS = 12
ROW_CHUNK = 256
VMEM_LIMIT = 56 * 1024 * 1024

F32 = jnp.float32
BF16 = jnp.bfloat16


def _cparams(sem):
    return pltpu.CompilerParams(dimension_semantics=sem, vmem_limit_bytes=VMEM_LIMIT)


def _resident(shape):
    zeros = (0,) * len(shape)
    return pl.BlockSpec(shape, lambda *_: zeros, pipeline_mode=pl.Buffered(1))


def _rmsnorm(x, g):
    return x * lax.rsqrt(jnp.mean(x * x, axis=-1, keepdims=True) + EPS) * g


def _modulate(y, mod_ref, off):
    return y * (1.0 + mod_ref[off + 1:off + 2, :]) + mod_ref[off:off + 1, :]


def _dot(a, b):
    return jnp.dot(a, b, preferred_element_type=F32)


def _dot_t(a, b):
    return lax.dot_general(a, b, (((1,), (1,)), ((), ())), preferred_element_type=F32)


def _mod_kernel(cc_ref, w_ref, b_ref, o_ref):
    s = cc_ref[...]
    s = s * jax.nn.sigmoid(s)
    o_ref[...] = _dot(s.astype(BF16), w_ref[...].astype(BF16)) + b_ref[...]


def _modulation(cc, mod_w, mod_b):
    depth, d, n = mod_w.shape
    tn = n // 4
    out = pl.pallas_call(
        _mod_kernel,
        grid=(depth, n // tn),
        in_specs=[
            pl.BlockSpec((MOD_ROWS, d), lambda l, j: (0, 0)),
            pl.BlockSpec((None, d, tn), lambda l, j: (l, 0, j)),
            pl.BlockSpec((None, 1, tn), lambda l, j: (l, 0, j)),
        ],
        out_specs=pl.BlockSpec((None, MOD_ROWS, tn), lambda l, j: (l, 0, j)),
        out_shape=jax.ShapeDtypeStruct((depth, MOD_ROWS, n), F32),
        compiler_params=_cparams(("parallel", "parallel")),
        name="modulation",
    )(cc, mod_w, mod_b.reshape(depth, 1, n))
    return out.reshape(depth, MOD_ROWS, N_MOD, d)


def _in_proj_kernel(x_ref, mod_ref, g_ref, w_ref, *rest, has_bias, has_vt, write_out, tm):
    rest = list(rest)
    b_ref = rest.pop(0) if has_bias else None
    wvt_ref = rest.pop(0) if has_vt else None
    vt_ref = rest.pop() if has_vt else None
    out_refs = rest
    for c in range(tm // ROW_CHUNK):
        rows = slice(c * ROW_CHUNK, (c + 1) * ROW_CHUNK)
        a = _modulate(_rmsnorm(x_ref[rows, :], g_ref[...]), mod_ref, 0).astype(BF16)
        acc = _dot(a, w_ref[...])
        if has_bias:
            acc = acc + b_ref[...]
        write_out(acc, out_refs, rows)
        if has_vt:
            vt_ref[c] = _dot_t(wvt_ref[...], a).astype(vt_ref.dtype)


def _in_proj(x, mod, gain, w, bias, *, layer, n_rows, mod_row, tm, out_widths, out_dtypes,
             write_out, name, wvt=None):
    d, n = w.shape
    in_specs = [
        pl.BlockSpec((tm, d), lambda i: (i, 0)),
        pl.BlockSpec((None, None, N_MOD, d), lambda i: (layer, mod_row(i), 0, 0)),
        pl.BlockSpec((None, 1, d), lambda i: (layer, 0, 0)),
        _resident((d, n)),
    ]
    args = [x, mod, gain, w]
    if bias is not None:
        in_specs.append(_resident((1, n)))
        args.append(bias)
    out_specs = [pl.BlockSpec((tm, wd), lambda i: (i, 0)) for wd in out_widths]
    out_shape = [jax.ShapeDtypeStruct((n_rows, wd), dt) for wd, dt in zip(out_widths, out_dtypes)]
    if wvt is not None:
        in_specs.append(_resident((d, d)))
        args.append(wvt)
        n_ch = tm // ROW_CHUNK
        out_specs.append(pl.BlockSpec((n_ch, d, ROW_CHUNK), lambda i: (i, 0, 0)))
        out_shape.append(jax.ShapeDtypeStruct((n_rows // ROW_CHUNK, d, ROW_CHUNK), BF16))
    kern = functools.partial(_in_proj_kernel, has_bias=bias is not None, has_vt=wvt is not None,
                             write_out=write_out, tm=tm)
    return pl.pallas_call(
        kern,
        grid=(n_rows // tm,),
        in_specs=in_specs,
        out_specs=out_specs,
        out_shape=out_shape,
        compiler_params=_cparams(("parallel",)),
        name=name,
    )(*args)


def _write_qkv(d, scale):
    def write(acc, out_refs, rows):
        (o_ref,) = out_refs
        o_ref[rows, :d] = (acc[:, :d] * scale).astype(o_ref.dtype)
        o_ref[rows, d:] = acc[:, d:].astype(o_ref.dtype)
    return write


def _write_glu(d):
    def write(acc, out_refs, rows):
        (o_ref,) = out_refs
        o_ref[rows, :] = acc[:, :d] * jax.nn.sigmoid(acc[:, d:])
    return write


def _write_gated(d):
    def write(acc, out_refs, rows):
        bg_ref, g_ref = out_refs
        bg_ref[rows, :] = acc[:, :d]
        g_ref[rows, :] = acc[:, d:2 * d] * acc[:, 2 * d:]
    return write


def _attn_geometry(rows):
    kh = min(WIN_ROWS_MAX, rows)
    rq, kr = ATTN_Q_ROWS, ATTN_K_ROWS
    assert rows % rq == 0 and rows >= kr and kr % rq == 0
    n_blk = rows // rq
    r_ar = np.arange(rows)
    row_start = np.clip(r_ar - kh // 2, 0, rows - kh)
    c_ar = np.arange(GRID_W)
    col_start = np.clip(c_ar - WIN_COLS // 2, 0, GRID_W - WIN_COLS)
    key_chunk = np.zeros(n_blk, np.int32)
    cls_of = np.zeros(n_blk, np.int32)
    patterns = []
    for blk in range(n_blk):
        r = np.arange(blk * rq, (blk + 1) * rq)
        kb = int(np.clip(row_start[r[0]] // rq * rq, 0, rows - kr))
        assert row_start[r].min() >= kb and row_start[r].max() + kh <= kb + kr
        key_chunk[blk] = kb // rq
        pat = (tuple(r - kb), tuple(row_start[r] - kb))
        if pat not in patterns:
            patterns.append(pat)
        cls_of[blk] = patterns.index(pat)
    n_rpb_rows = 2 * WIN_ROWS_MAX - 1
    row_idx = np.full((len(patterns), kr, rq), n_rpb_rows, np.int32)
    for ci, (r_rel, rs_rel) in enumerate(patterns):
        for qi in range(rq):
            for ki in range(kr):
                if rs_rel[qi] <= ki < rs_rel[qi] + kh:
                    row_idx[ci, ki, qi] = ki - r_rel[qi] + WIN_ROWS_MAX - 1
    col_ok = ((c_ar[:, None] >= col_start[None, :])
              & (c_ar[:, None] < col_start[None, :] + WIN_COLS))
    col_off = c_ar[:, None] - c_ar[None, :] + WIN_COLS - 1
    onehot = np.zeros((2 * WIN_COLS - 1, GRID_W, GRID_W), np.float32)
    kc_i, c_i = np.nonzero(col_ok)
    onehot[col_off[kc_i, c_i], kc_i, c_i] = 1.0
    return dict(rq=rq, kr=kr, n_blk=n_blk, key_chunk=key_chunk, cls_of=cls_of,
                row_idx=row_idx, col_ok=col_ok, onehot=onehot)


def _bias_table_kernel(toe_ref, o_ref, *, row_idx):
    n_cls, kr, rq = row_idx.shape
    for ci in range(n_cls):
        for ki in range(kr):
            o_ref[ci, ki * GRID_W:(ki + 1) * GRID_W, :] = jnp.concatenate(
                [toe_ref[int(row_idx[ci, ki, qi])] for qi in range(rq)], axis=1)


def _attn_bias_table(rpb, geo):
    h = rpb.shape[0]
    toe = jnp.einsum("hrj,jkc->hrkc", rpb, jnp.asarray(geo["onehot"]),
                     precision=lax.Precision.HIGHEST)
    toe = jnp.where(geo["col_ok"][None, None], toe * LOG2E, MASK_BIAS)
    ext = jnp.concatenate([toe, jnp.full((h, 1, GRID_W, GRID_W), MASK_BIAS, F32)], axis=1)
    n_cls, kr, rq = geo["row_idx"].shape
    return pl.pallas_call(
        functools.partial(_bias_table_kernel, row_idx=geo["row_idx"]),
        grid=(h,),
        in_specs=[pl.BlockSpec((None,) + ext.shape[1:], lambda i: (i, 0, 0, 0))],
        out_specs=pl.BlockSpec((None, n_cls, kr * GRID_W, rq * GRID_W), lambda i: (i, 0, 0, 0)),
        out_shape=jax.ShapeDtypeStruct((h, n_cls, kr * GRID_W, rq * GRID_W), F32),
        compiler_params=_cparams(("parallel",)),
        name="attn_bias_table",
    )(ext)


def _attn_lat_kernel(kch_ref, cls_ref, q_ref, k_ref, v_ref, kc_ref, vc_ref, bias_ref, *rest,
                     qn, kn, n_blk, vt_in, mxu_xpose, pipelined, aliased):
    rest = list(rest)
    if aliased:
        rest.pop(0)
    o_ref = rest.pop(0)
    if vt_in:
        kcat_scr, s_scr, m_scr = rest
    else:
        vt_scr, kcat_scr, s_scr, m_scr = rest
    seq = q_ref.shape[0]
    n_vch = seq // qn
    n_kch = kn // qn
    lane = lax.broadcasted_iota(jnp.int32, (1, LANES), 1)
    sub = lax.broadcasted_iota(jnp.int32, (LANES, 1), 0)
    lane_lo = lane < (LANES // 2)
    sub_lo = sub < (LANES // 2)
    if vt_in:
        vt_chunk = lambda j: v_ref[j]
        vt_ctx = lambda: vc_ref[0]
    else:
        for j in range(n_vch):
            vt_scr[j] = v_ref[j * qn:(j + 1) * qn, :].astype(F32).T.astype(BF16)
        vt_scr[n_vch] = vc_ref[...].astype(F32).T.astype(BF16)
        vt_chunk = lambda j: vt_scr[j]
        vt_ctx = lambda: vt_scr[n_vch]
    kcat_scr[kn:, :] = kc_ref[...]
    if mxu_xpose:
        eye = (lax.broadcasted_iota(jnp.int32, (qn, qn), 0)
               == lax.broadcasted_iota(jnp.int32, (qn, qn), 1)).astype(BF16)

    def stage_a(blk, slot):
        q0 = pl.multiple_of(blk * qn, qn)
        k0 = pl.multiple_of(kch_ref[blk] * qn, qn)
        cls = cls_ref[blk]
        kcat_scr[0:kn, :] = k_ref[pl.ds(k0, kn), :]
        q = q_ref[pl.ds(q0, qn), :]
        kcat = kcat_scr[...]
        for h, sel in enumerate((lane_lo, jnp.logical_not(lane_lo))):
            qm = jnp.where(sel, q, jnp.zeros_like(q))
            s = _dot_t(kcat, qm)
            s_lat = s[:kn, :] + bias_ref[h, cls]
            s_ctx = s[kn:, :]
            s_scr[slot, h, 0:kn, :] = s_lat
            s_scr[slot, h, kn:, :] = s_ctx
            m = jnp.maximum(jnp.max(s_lat, axis=0, keepdims=True),
                            jnp.max(s_ctx, axis=0, keepdims=True))
            m_scr[slot, h] = jnp.broadcast_to(m, (8, qn))

    def stage_b(blk, slot):
        q0 = pl.multiple_of(blk * qn, qn)
        kch = kch_ref[blk]
        vt = jnp.concatenate([vt_chunk(kch + j) for j in range(n_kch)] + [vt_ctx()], axis=1)
        outs = []
        for h in range(2):
            m = m_scr[slot, h][0:1, :]
            p = jnp.exp2(s_scr[slot, h] - m)
            l = jnp.sum(p, axis=0, keepdims=True)
            o_t = _dot(vt, p.astype(BF16))
            outs.append(o_t * (1.0 / l))
        o_t = jnp.where(sub_lo, outs[0], outs[1])
        if mxu_xpose:
            o = _dot_t(eye, o_t.astype(BF16))
        else:
            o = o_t.T
        o_ref[pl.ds(q0, qn), :] = o.astype(o_ref.dtype)

    if pipelined:
        stage_a(0, 0)

        def body(blk, carry):
            slot = lax.rem(blk, 2)
            stage_a(blk, slot)
            stage_b(blk - 1, 1 - slot)
            return carry

        lax.fori_loop(1, n_blk, body, 0)
        stage_b(n_blk - 1, (n_blk - 1) % 2)
    else:
        def body(blk, carry):
            stage_a(blk, 0)
            stage_b(blk, 0)
            return carry

        lax.fori_loop(0, n_blk, body, 0)


def _attn_rows_kernel(kch_ref, cls_ref, q_ref, k_ref, v_ref, kc_ref, vc_ref, bias_ref, *rest,
                      qn, kn, n_blk, aliased):
    o_ref = rest[1] if aliased else rest[0]
    lane = lax.broadcasted_iota(jnp.int32, (1, LANES), 1)
    lane_lo = lane < (LANES // 2)
    kc = kc_ref[...]
    vc = vc_ref[...]

    def body(blk, carry):
        q0 = pl.multiple_of(blk * qn, qn)
        k0 = pl.multiple_of(kch_ref[blk] * qn, qn)
        cls = cls_ref[blk]
        q = q_ref[pl.ds(q0, qn), :]
        kw = k_ref[pl.ds(k0, kn), :]
        vw = v_ref[pl.ds(k0, kn), :]
        outs = []
        for h, sel in enumerate((lane_lo, jnp.logical_not(lane_lo))):
            qm = jnp.where(sel, q, jnp.zeros_like(q))
            s_lat = _dot_t(qm, kw) + bias_ref[h, cls]
            s_ctx = _dot_t(qm, kc)
            m = jnp.maximum(jnp.max(s_lat, axis=-1, keepdims=True),
                            jnp.max(s_ctx, axis=-1, keepdims=True))
            p_lat = jnp.exp2(s_lat - m)
            p_ctx = jnp.exp2(s_ctx - m)
            l = jnp.sum(p_lat, axis=-1, keepdims=True) + jnp.sum(p_ctx, axis=-1, keepdims=True)
            o = _dot(p_lat.astype(BF16), vw) + _dot(p_ctx.astype(BF16), vc)
            outs.append(o * (1.0 / l))
        o_ref[pl.ds(q0, qn), :] = jnp.where(lane_lo, outs[0], outs[1]).astype(o_ref.dtype)
        return carry

    lax.fori_loop(0, n_blk, body, 0)


def _attention_latent(qkv, vt3, bias_tab, geo, o_prev, *, variant, hp0, n_hp_call, batch, seq,
                      ctx_len, d, n_out_rows):
    n_hp = d // LANES
    t_lat = batch * seq
    qn, kn = geo["rq"] * GRID_W, geo["kr"] * GRID_W
    assert ctx_len == qn
    n_cls = bias_tab.shape[1]
    ctx_blk0 = t_lat // ctx_len
    vt_in = variant == "pipe_vt"
    rows_variant = variant == "rows"
    in_specs = [
        pl.BlockSpec((seq, LANES), lambda hp, b, *_: (b, hp0 + hp)),
        pl.BlockSpec((seq, LANES), lambda hp, b, *_: (b, n_hp + hp0 + hp)),
    ]
    args = [qkv, qkv]
    if vt_in:
        in_specs.append(pl.BlockSpec((seq // qn, LANES, qn), lambda hp, b, *_: (b, hp0 + hp, 0)))
        args.append(vt3)
    else:
        in_specs.append(pl.BlockSpec((seq, LANES), lambda hp, b, *_: (b, 2 * n_hp + hp0 + hp)))
        args.append(qkv)
    in_specs.append(pl.BlockSpec((ctx_len, LANES), lambda hp, b, *_: (ctx_blk0 + b, n_hp + hp0 + hp)))
    args.append(qkv)
    if vt_in:
        in_specs.append(pl.BlockSpec((1, LANES, qn), lambda hp, b, *_: (ctx_blk0 + b, hp0 + hp, 0)))
        args.append(vt3)
    else:
        in_specs.append(pl.BlockSpec((ctx_len, LANES),
                                     lambda hp, b, *_: (ctx_blk0 + b, 2 * n_hp + hp0 + hp)))
        args.append(qkv)
    if rows_variant:
        in_specs.append(pl.BlockSpec((2, n_cls, qn, kn), lambda hp, b, *_: ((hp0 + hp), 0, 0, 0)))
    else:
        in_specs.append(pl.BlockSpec((2, n_cls, kn, qn), lambda hp, b, *_: ((hp0 + hp), 0, 0, 0)))
    args.append(bias_tab)
    aliases = {}
    if o_prev is not None:
        in_specs.append(pl.BlockSpec(memory_space=pl.ANY))
        args.append(o_prev)
        aliases = {2 + len(args) - 1: 0}
    if rows_variant:
        scratch = []
        kern = functools.partial(_attn_rows_kernel, qn=qn, kn=kn, n_blk=geo["n_blk"],
                                 aliased=o_prev is not None)
    else:
        scratch = [pltpu.VMEM((kn + ctx_len, LANES), BF16),
                   pltpu.VMEM((2, 2, kn + ctx_len, qn), F32),
                   pltpu.VMEM((2, 2, 8, qn), F32)]
        if not vt_in:
            scratch.insert(0, pltpu.VMEM((seq // qn + 1, LANES, qn), BF16))
        kern = functools.partial(_attn_lat_kernel, qn=qn, kn=kn, n_blk=geo["n_blk"], vt_in=vt_in,
                                 mxu_xpose=vt_in, pipelined=variant != "nopipe",
                                 aliased=o_prev is not None)
    grid_spec = pltpu.PrefetchScalarGridSpec(
        num_scalar_prefetch=2,
        grid=(n_hp_call, batch),
        in_specs=in_specs,
        out_specs=pl.BlockSpec((seq, LANES), lambda hp, b, *_: (b, hp0 + hp)),
        scratch_shapes=scratch,
    )
    return pl.pallas_call(
        kern,
        grid_spec=grid_spec,
        out_shape=jax.ShapeDtypeStruct((n_out_rows, d), BF16),
        input_output_aliases=aliases,
        compiler_params=_cparams(("parallel", "parallel")),
        name="attn_" + variant,
    )(jnp.asarray(geo["key_chunk"]), jnp.asarray(geo["cls_of"]), *args)


def _attn_ctx_kernel(q_ref, k_ref, v_ref, o_in_ref, o_ref):
    del o_in_ref
    lane = lax.broadcasted_iota(jnp.int32, (1, LANES), 1)
    lane_lo = lane < (LANES // 2)
    q = q_ref[...]
    k = k_ref[...]
    v = v_ref[...]
    outs = []
    for sel in (lane_lo, jnp.logical_not(lane_lo)):
        qm = jnp.where(sel, q, jnp.zeros_like(q))
        s = _dot_t(qm, k)
        p = jnp.exp2(s - jnp.max(s, axis=-1, keepdims=True))
        l = jnp.sum(p, axis=-1, keepdims=True)
        outs.append(_dot(p.astype(BF16), v) / l)
    o_ref[...] = jnp.where(lane_lo, outs[0], outs[1]).astype(o_ref.dtype)


def _attention_ctx(qkv, o_all, *, batch, seq, ctx_len, d):
    n_hp = d // LANES
    blk0 = batch * seq // ctx_len
    return pl.pallas_call(
        _attn_ctx_kernel,
        grid=(batch, n_hp),
        in_specs=[
            pl.BlockSpec((ctx_len, LANES), lambda b, hp: (blk0 + b, hp)),
            pl.BlockSpec((ctx_len, LANES), lambda b, hp: (blk0 + b, n_hp + hp)),
            pl.BlockSpec((ctx_len, LANES), lambda b, hp: (blk0 + b, 2 * n_hp + hp)),
            pl.BlockSpec(memory_space=pl.ANY),
        ],
        out_specs=pl.BlockSpec((ctx_len, LANES), lambda b, hp: (blk0 + b, hp)),
        out_shape=jax.ShapeDtypeStruct(o_all.shape, o_all.dtype),
        input_output_aliases={3: 0},
        compiler_params=_cparams(("parallel", "parallel")),
        name="attn_ctx",
    )(qkv, qkv, qkv, o_all)


def _chunk_with_halo(prev_ref, cur_ref, next_ref, c, n_chunks, halo, t0, t_lat, seq, ctx_len):
    seq_len = jnp.where(t0 < t_lat, seq, ctx_len)
    lo = c * ROW_CHUNK
    starts = lax.rem(t0 + lo, seq_len) == 0
    ends = lax.rem(t0 + lo + ROW_CHUNK, seq_len) == 0
    prev = prev_ref[...] if c == 0 else cur_ref[lo - halo:lo, :]
    nxt = next_ref[...] if c == n_chunks - 1 else cur_ref[lo + ROW_CHUNK:lo + ROW_CHUNK + halo, :]
    prev = jnp.where(starts, jnp.zeros_like(prev), prev)
    nxt = jnp.where(ends, jnp.zeros_like(nxt), nxt)
    return prev, cur_ref[lo:lo + ROW_CHUNK, :], nxt


CONF_HALO = 16
CONF_ROW_CHUNK = 64
SC_HALO = 8


def _conformer_core(up_ref, uc_ref, un_ref, dw_ref, dwb_ref, lng_ref, lnb_ref, scr, cv, *,
                    c, n_chunks, t0, t_lat, seq, ctx_len):
    n_lg, taps = dw_ref.shape[0], dw_ref.shape[1]
    prev, cur, nxt = _chunk_with_halo(up_ref, uc_ref, un_ref, c, n_chunks, CONF_HALO, t0, t_lat,
                                      seq, ctx_len)
    for g in range(n_lg):
        cols = slice(g * LANES, (g + 1) * LANES)
        scr[c, g, 0:CONF_HALO, :] = prev[:, cols]
        scr[c, g, CONF_HALO:CONF_HALO + ROW_CHUNK, :] = cur[:, cols]
        scr[c, g, CONF_HALO + ROW_CHUNK:, :] = nxt[:, cols]
    base = CONF_HALO - (taps - 1) // 2
    for g in range(n_lg):
        w_g = dw_ref[g]
        b_g = dwb_ref[g]
        for r0 in range(0, ROW_CHUNK, CONF_ROW_CHUNK):
            acc = jnp.broadcast_to(b_g, (CONF_ROW_CHUNK, LANES))
            for k in range(taps):
                acc = acc + w_g[k:k + 1, :] * scr[c, g, r0 + base + k:r0 + base + k + CONF_ROW_CHUNK, :]
            cv[c, g, r0:r0 + CONF_ROW_CHUNK, :] = acc
    u = jnp.concatenate([cv[c, g] for g in range(n_lg)], axis=-1)
    mu = jnp.mean(u, axis=-1, keepdims=True)
    uc = u - mu
    var = jnp.mean(uc * uc, axis=-1, keepdims=True)
    y = uc * lax.rsqrt(var + EPS) * lng_ref[...] + lnb_ref[...]
    return (y * jax.nn.sigmoid(y)).astype(BF16)


def _shortconv_core(gp_ref, gc_ref, gn_ref, bg_ref, cw_ref, scr, *, c, n_chunks, t0, t_lat, seq,
                    ctx_len):
    prev, cur, nxt = _chunk_with_halo(gp_ref, gc_ref, gn_ref, c, n_chunks, SC_HALO, t0, t_lat,
                                      seq, ctx_len)
    scr[c, 0:SC_HALO, :] = prev
    scr[c, SC_HALO:SC_HALO + ROW_CHUNK, :] = cur
    scr[c, SC_HALO + ROW_CHUNK:, :] = nxt
    conv = (cw_ref[0:1, :] * scr[c, SC_HALO - 1:SC_HALO - 1 + ROW_CHUNK, :]
            + cw_ref[1:2, :] * scr[c, SC_HALO:SC_HALO + ROW_CHUNK, :]
            + cw_ref[2:3, :] * scr[c, SC_HALO + 1:SC_HALO + 1 + ROW_CHUNK, :])
    lo = c * ROW_CHUNK
    return (bg_ref[lo:lo + ROW_CHUNK, :] * conv).astype(BF16)


def _post_mixer_kernel(*refs, variant, n_in, n_scr, has_bias, final, tm, t_lat, seq, ctx_len):
    mixer_refs = refs[:n_in]
    x_ref, mod_ref, g2_ref, wout_ref, w1_ref, w2_ref = refs[n_in:n_in + 6]
    rest = list(refs[n_in + 6:])
    bout_ref = rest.pop(0) if has_bias else None
    fg_ref = rest.pop(0) if final else None
    o_ref = rest.pop(0)
    scr_refs = rest
    assert len(scr_refs) == n_scr

    def tail(a, rows):
        y = _dot(a, wout_ref[...])
        if has_bias:
            y = y + bout_ref[...]
        x1 = x_ref[rows, :] + mod_ref[2:3, :] * y
        m = _modulate(_rmsnorm(x1, g2_ref[...]), mod_ref, 3).astype(BF16)
        h = jnp.maximum(_dot(m, w1_ref[...]), 0.0)
        h = (h * h).astype(BF16)
        x2 = x1 + mod_ref[5:6, :] * _dot(h, w2_ref[...])
        if final:
            x2 = _rmsnorm(x2, fg_ref[...])
        o_ref[rows, :] = x2

    n_chunks = tm // ROW_CHUNK
    t0 = pl.program_id(0) * tm
    for c in range(n_chunks):
        rows = slice(c * ROW_CHUNK, (c + 1) * ROW_CHUNK)
        if variant == "plain":
            a = mixer_refs[0][rows, :]
        else:
            core = _conformer_core if variant == "conformer" else _shortconv_core
            a = core(*mixer_refs, *scr_refs, c=c, n_chunks=n_chunks, t0=t0, t_lat=t_lat, seq=seq,
                     ctx_len=ctx_len)
        tail(a, rows)


def _post_mixer(variant, mixer_args, mixer_specs, scratch, x, mod, g2, wout, bout, w1, w2,
                final_g, *, layer, n_rows, mod_row, tm, t_lat, seq, ctx_len):
    d = x.shape[1]
    f = w1.shape[1]
    in_specs = list(mixer_specs) + [
        pl.BlockSpec((tm, d), lambda i: (i, 0)),
        pl.BlockSpec((None, None, N_MOD, d), lambda i: (layer, mod_row(i), 0, 0)),
        pl.BlockSpec((None, 1, d), lambda i: (layer, 0, 0)),
        _resident((d, d)),
        _resident((d, f)),
        _resident((f, d)),
    ]
    args = list(mixer_args) + [x, mod, g2, wout, w1, w2]
    if bout is not None:
        in_specs.append(_resident((1, d)))
        args.append(bout)
    if final_g is not None:
        in_specs.append(_resident((1, d)))
        args.append(final_g)
    kern = functools.partial(_post_mixer_kernel, variant=variant, n_in=len(mixer_args),
                             n_scr=len(scratch), has_bias=bout is not None,
                             final=final_g is not None, tm=tm, t_lat=t_lat, seq=seq,
                             ctx_len=ctx_len)
    return pl.pallas_call(
        kern,
        grid=(n_rows // tm,),
        in_specs=in_specs,
        out_specs=pl.BlockSpec((tm, d), lambda i: (i, 0)),
        out_shape=jax.ShapeDtypeStruct((n_rows, d), F32),
        scratch_shapes=list(scratch),
        compiler_params=_cparams(("parallel",)),
        name="post_mixer_" + variant,
    )(*args)


def _halo_specs(n_total_rows, tm, halo, d):
    hb = tm // halo
    n_hblk = n_total_rows // halo
    return [
        pl.BlockSpec((halo, d), lambda i: (jnp.maximum(i * hb - 1, 0), 0)),
        pl.BlockSpec((tm, d), lambda i: (i, 0)),
        pl.BlockSpec((halo, d), lambda i: (jnp.minimum((i + 1) * hb, n_hblk - 1), 0)),
    ]


def kernel(x, c, ctx, c_ctx, mod_w, mod_b, norm1_g, norm2_g, mlp_w1, mlp_w2, na_wqkv, na_wo,
           na_rpb, cv_w1, cv_b1, cv_dw, cv_dwb, cv_ln_g, cv_ln_b, cv_w2, cv_b2, sc_win, sc_conv,
           sc_wout, final_g):
    batch, seq, d = x.shape
    ctx_len = ctx.shape[1]
    depth = mod_w.shape[0]
    t_lat = batch * seq
    t_all = t_lat + batch * ctx_len
    head_dim = d // N_HEADS
    assert 2 * head_dim == LANES and batch < MOD_ROWS and seq % GRID_W == 0
    rows = seq // GRID_W
    tm = 2 * ROW_CHUNK
    n_chunks = tm // ROW_CHUNK
    assert seq % tm == 0 and (batch * ctx_len) % tm == 0
    assert seq % ROW_CHUNK == 0 and ctx_len % ROW_CHUNK == 0
    last_attn = max(i for i in range(depth) if i % N_MIXERS == 0)
    n_lg = d // LANES

    def mod_row_fn(tile):
        def f(i):
            return jnp.where(i * tile < t_lat, (i * tile) // seq, batch)
        return f

    cc = jnp.concatenate([c, c_ctx[None], jnp.zeros((MOD_ROWS - batch - 1, d), F32)], axis=0)
    mod = _modulation(cc, mod_w, mod_b)

    tok = jnp.concatenate([x.reshape(t_lat, d), ctx.reshape(batch * ctx_len, d)], axis=0)
    g1 = norm1_g.reshape(depth, 1, d)
    g2 = norm2_g.reshape(depth, 1, d)
    geo = _attn_geometry(rows)

    for i in range(depth):
        kind = i % N_MIXERS
        slot = i // N_MIXERS
        ctx_live = i < last_attn
        n_rows = t_all if ctx_live else t_lat
        inp = functools.partial(_in_proj, tok, mod, g1, layer=i, mod_row=mod_row_fn(tm), tm=tm)
        post = functools.partial(
            _post_mixer, x=tok, mod=mod, g2=g2, w1=mlp_w1[i].astype(BF16),
            w2=mlp_w2[i].astype(BF16), final_g=final_g[None] if i == depth - 1 else None,
            layer=i, n_rows=n_rows, t_lat=t_lat, seq=seq, ctx_len=ctx_len)
        if kind == 0:
            wqkv = na_wqkv[slot]
            qkv, vt3 = inp(wqkv.astype(BF16), None, n_rows=t_all, out_widths=[3 * d],
                           out_dtypes=[BF16], write_out=_write_qkv(d, head_dim ** -0.5 * LOG2E),
                           name="in_proj_qkv", wvt=wqkv[:, 2 * d:].T.astype(BF16))
            bias_tab = _attn_bias_table(na_rpb[slot], geo)
            variants = ("pipe", "pipe_vt") if i == 0 else ("nopipe", "rows")
            half = d // LANES // 2
            o = None
            for vi, variant in enumerate(variants):
                tab = bias_tab.transpose(0, 1, 3, 2) if variant == "rows" else bias_tab
                o = _attention_latent(qkv, vt3, tab, geo, o, variant=variant, hp0=vi * half,
                                      n_hp_call=half, batch=batch, seq=seq, ctx_len=ctx_len,
                                      d=d, n_out_rows=n_rows)
            if ctx_live:
                o = _attention_ctx(qkv, o, batch=batch, seq=seq, ctx_len=ctx_len, d=d)
            tok = post("plain", [o], [pl.BlockSpec((tm, d), lambda r: (r, 0))], [],
                       wout=na_wo[slot].astype(BF16), bout=None, mod_row=mod_row_fn(tm), tm=tm)
        elif kind == 1:
            (u,) = inp(cv_w1[slot].astype(BF16), cv_b1[slot][None], n_rows=n_rows,
                       out_widths=[d], out_dtypes=[F32], write_out=_write_glu(d),
                       name="in_proj_glu")
            taps = cv_dw.shape[1]
            dw = cv_dw[slot].reshape(taps, n_lg, LANES).transpose(1, 0, 2)
            args = [u, u, u, dw, cv_dwb[slot].reshape(n_lg, 1, LANES), cv_ln_g[slot][None],
                    cv_ln_b[slot][None]]
            specs = _halo_specs(n_rows, tm, CONF_HALO, d) + [
                _resident((n_lg, taps, LANES)), _resident((n_lg, 1, LANES)),
                _resident((1, d)), _resident((1, d))]
            scratch = [pltpu.VMEM((n_chunks, n_lg, ROW_CHUNK + 2 * CONF_HALO, LANES), F32),
                       pltpu.VMEM((n_chunks, n_lg, ROW_CHUNK, LANES), F32)]
            tok = post("conformer", args, specs, scratch, wout=cv_w2[slot].astype(BF16),
                       bout=cv_b2[slot][None], mod_row=mod_row_fn(tm), tm=tm)
        else:
            bg, g = inp(sc_win[slot].astype(BF16), None, n_rows=n_rows, out_widths=[d, d],
                        out_dtypes=[F32, F32], write_out=_write_gated(d), name="in_proj_gated")
            args = [g, g, g, bg, sc_conv[slot]]
            specs = _halo_specs(n_rows, tm, SC_HALO, d) + [
                pl.BlockSpec((tm, d), lambda r: (r, 0)), _resident(sc_conv[slot].shape)]
            scratch = [pltpu.VMEM((n_chunks, ROW_CHUNK + 2 * SC_HALO, d), F32)]
            tok = post("shortconv", args, specs, scratch, wout=sc_wout[slot].astype(BF16),
                       bout=None, mod_row=mod_row_fn(tm), tm=tm)
    return tok[:t_lat].reshape(batch, seq, d)
```

```python
import functools

import numpy as np
import jax
import jax.numpy as jnp
from jax import lax
from jax.experimental import pallas as pl
from jax.experimental.pallas import tpu as pltpu

N_HEADS = 16
GRID_W = 64
WIN_ROWS_MAX = 8
WIN_COLS = 16
N_MIXERS = 3
N_MOD = 6
EPS = 1e-6
MASK_BIAS = -1e30
LOG2E = 1.4426950408889634

LANES = 128
MOD_ROWS = 16
ATTN_Q_ROWS = 4
ATTN_K_ROWS = 12
ROW_CHUNK = 256
VMEM_LIMIT = 56 * 1024 * 1024

F32 = jnp.float32
BF16 = jnp.bfloat16


def _cparams(sem):
    return pltpu.CompilerParams(dimension_semantics=sem, vmem_limit_bytes=VMEM_LIMIT)


def _resident(shape):
    zeros = (0,) * len(shape)
    return pl.BlockSpec(shape, lambda *_: zeros, pipeline_mode=pl.Buffered(1))


def _rmsnorm(x, g):
    return x * lax.rsqrt(jnp.mean(x * x, axis=-1, keepdims=True) + EPS) * g


def _modulate(y, mod_ref, off):
    return y * (1.0 + mod_ref[off + 1:off + 2, :]) + mod_ref[off:off + 1, :]


def _dot(a, b):
    return jnp.dot(a, b, preferred_element_type=F32)


def _dot_t(a, b):
    return lax.dot_general(a, b, (((1,), (1,)), ((), ())), preferred_element_type=F32)


def _mod_kernel(cc_ref, w_ref, b_ref, o_ref):
    s = cc_ref[...]
    s = s * jax.nn.sigmoid(s)
    o_ref[...] = _dot(s.astype(BF16), w_ref[...].astype(BF16)) + b_ref[...]


def _modulation(cc, mod_w, mod_b):
    depth, d, n = mod_w.shape
    tn = n // 4
    out = pl.pallas_call(
        _mod_kernel,
        grid=(depth, n // tn),
        in_specs=[
            pl.BlockSpec((MOD_ROWS, d), lambda l, j: (0, 0)),
            pl.BlockSpec((None, d, tn), lambda l, j: (l, 0, j)),
            pl.BlockSpec((None, 1, tn), lambda l, j: (l, 0, j)),
        ],
        out_specs=pl.BlockSpec((None, MOD_ROWS, tn), lambda l, j: (l, 0, j)),
        out_shape=jax.ShapeDtypeStruct((depth, MOD_ROWS, n), F32),
        compiler_params=_cparams(("parallel", "parallel")),
        name="modulation",
    )(cc, mod_w, mod_b.reshape(depth, 1, n))
    return out.reshape(depth, MOD_ROWS, N_MOD, d)


def _in_proj_kernel(x_ref, mod_ref, g_ref, w_ref, *rest, has_bias, write_out, tm):
    b_ref = rest[0] if has_bias else None
    out_refs = rest[1:] if has_bias else rest
    for c in range(tm // ROW_CHUNK):
        rows = slice(c * ROW_CHUNK, (c + 1) * ROW_CHUNK)
        a = _modulate(_rmsnorm(x_ref[rows, :], g_ref[...]), mod_ref, 0).astype(BF16)
        acc = _dot(a, w_ref[...])
        if has_bias:
            acc = acc + b_ref[...]
        write_out(acc, out_refs, rows)


def _in_proj(x, mod, gain, w, bias, *, layer, n_rows, mod_row, tm, out_specs, out_shape,
             write_out, name):
    d, n = w.shape
    in_specs = [
        pl.BlockSpec((tm, d), lambda i: (i, 0)),
        pl.BlockSpec((None, None, N_MOD, d), lambda i: (layer, mod_row(i), 0, 0)),
        pl.BlockSpec((None, 1, d), lambda i: (layer, 0, 0)),
        _resident((d, n)),
    ]
    args = [x, mod, gain, w]
    if bias is not None:
        in_specs.append(_resident((1, n)))
        args.append(bias)
    kern = functools.partial(_in_proj_kernel, has_bias=bias is not None, write_out=write_out,
                             tm=tm)
    return pl.pallas_call(
        kern,
        grid=(n_rows // tm,),
        in_specs=in_specs,
        out_specs=out_specs,
        out_shape=out_shape,
        compiler_params=_cparams(("parallel",)),
        name=name,
    )(*args)


def _row_tiles(n_rows, widths, dtypes, tm):
    return dict(out_specs=[pl.BlockSpec((tm, wd), lambda i: (i, 0)) for wd in widths],
                out_shape=[jax.ShapeDtypeStruct((n_rows, wd), dt)
                           for wd, dt in zip(widths, dtypes)])


def _lane_group_tiles(n_rows, n_groups, dtype, tm):
    return dict(out_specs=[pl.BlockSpec((n_groups, tm, LANES), lambda i: (0, i, 0))],
                out_shape=[jax.ShapeDtypeStruct((n_groups, n_rows, LANES), dtype)])


def _write_qkv(d, scale):
    def write(acc, out_refs, rows):
        (o_ref,) = out_refs
        for j in range(o_ref.shape[0]):
            blk = acc[:, j * LANES:(j + 1) * LANES]
            if j * LANES < d:
                blk = blk * scale
            o_ref[j, rows, :] = blk.astype(o_ref.dtype)
    return write


def _write_glu(d):
    def write(acc, out_refs, rows):
        (o_ref,) = out_refs
        o_ref[rows, :] = acc[:, :d] * jax.nn.sigmoid(acc[:, d:])
    return write


def _write_gated(d):
    def write(acc, out_refs, rows):
        bg_ref, g_ref = out_refs
        bg_ref[rows, :] = acc[:, :d]
        g_ref[rows, :] = acc[:, d:2 * d] * acc[:, 2 * d:]
    return write


def _attn_geometry(rows):
    kh = min(WIN_ROWS_MAX, rows)
    rq, kr = ATTN_Q_ROWS, ATTN_K_ROWS
    assert rows % rq == 0 and rows >= kr and kr % rq == 0
    n_blk = rows // rq
    r_ar = np.arange(rows)
    row_start = np.clip(r_ar - kh // 2, 0, rows - kh)
    c_ar = np.arange(GRID_W)
    col_start = np.clip(c_ar - WIN_COLS // 2, 0, GRID_W - WIN_COLS)
    key_chunk = np.zeros(n_blk, np.int32)
    cls_of = np.zeros(n_blk, np.int32)
    patterns = []
    for blk in range(n_blk):
        r = np.arange(blk * rq, (blk + 1) * rq)
        kb = int(np.clip(row_start[r[0]] // rq * rq, 0, rows - kr))
        assert row_start[r].min() >= kb and row_start[r].max() + kh <= kb + kr
        key_chunk[blk] = kb // rq
        pat = (tuple(r - kb), tuple(row_start[r] - kb))
        if pat not in patterns:
            patterns.append(pat)
        cls_of[blk] = patterns.index(pat)
    n_rpb_rows = 2 * WIN_ROWS_MAX - 1
    row_idx = np.full((len(patterns), kr, rq), n_rpb_rows, np.int32)
    for ci, (r_rel, rs_rel) in enumerate(patterns):
        for qi in range(rq):
            for ki in range(kr):
                if rs_rel[qi] <= ki < rs_rel[qi] + kh:
                    row_idx[ci, ki, qi] = ki - r_rel[qi] + WIN_ROWS_MAX - 1
    col_ok = ((c_ar[:, None] >= col_start[None, :])
              & (c_ar[:, None] < col_start[None, :] + WIN_COLS))
    col_off = c_ar[:, None] - c_ar[None, :] + WIN_COLS - 1
    onehot = np.zeros((2 * WIN_COLS - 1, GRID_W, GRID_W), np.float32)
    kc_i, c_i = np.nonzero(col_ok)
    onehot[col_off[kc_i, c_i], kc_i, c_i] = 1.0
    return dict(rq=rq, kr=kr, n_blk=n_blk, key_chunk=key_chunk, cls_of=cls_of,
                row_idx=row_idx, col_ok=col_ok, onehot=onehot)


def _bias_table_kernel(toe_ref, o_ref, *, row_idx):
    n_cls, kr, rq = row_idx.shape
    for ci in range(n_cls):
        for ki in range(kr):
            o_ref[ci, ki * GRID_W:(ki + 1) * GRID_W, :] = jnp.concatenate(
                [toe_ref[int(row_idx[ci, ki, qi])] for qi in range(rq)], axis=1)


def _attn_bias_table(rpb, geo):
    h = rpb.shape[0]
    toe = jnp.einsum("hrj,jkc->hrkc", rpb, jnp.asarray(geo["onehot"]),
                     precision=lax.Precision.HIGHEST)
    toe = jnp.where(geo["col_ok"][None, None], toe * LOG2E, MASK_BIAS)
    ext = jnp.concatenate([toe, jnp.full((h, 1, GRID_W, GRID_W), MASK_BIAS, F32)], axis=1)
    n_cls, kr, rq = geo["row_idx"].shape
    return pl.pallas_call(
        functools.partial(_bias_table_kernel, row_idx=geo["row_idx"]),
        grid=(h,),
        in_specs=[pl.BlockSpec((None,) + ext.shape[1:], lambda i: (i, 0, 0, 0))],
        out_specs=pl.BlockSpec((None, n_cls, kr * GRID_W, rq * GRID_W), lambda i: (i, 0, 0, 0)),
        out_shape=jax.ShapeDtypeStruct((h, n_cls, kr * GRID_W, rq * GRID_W), F32),
        compiler_params=_cparams(("parallel",)),
        name="attn_bias_table",
    )(ext)


def _attn_lat_kernel(kch_ref, cls_ref, q_ref, k_ref, v_ref, kc_ref, vc_ref, bias_ref, *rest,
                     qn, kn, n_blk, aliased):
    rest = list(rest)
    if aliased:
        rest.pop(0)
    o_ref, vt_scr, kcat_scr, s_scr, m_scr = rest
    seq = q_ref.shape[0]
    n_vch = seq // qn
    n_kch = kn // qn
    lane = lax.broadcasted_iota(jnp.int32, (1, LANES), 1)
    sub = lax.broadcasted_iota(jnp.int32, (LANES, 1), 0)
    lane_lo = lane < (LANES // 2)
    sub_lo = sub < (LANES // 2)
    for j in range(n_vch):
        vt_scr[j] = v_ref[j * qn:(j + 1) * qn, :].astype(F32).T.astype(BF16)
    vt_scr[n_vch] = vc_ref[...].astype(F32).T.astype(BF16)
    kcat_scr[kn:, :] = kc_ref[...]

    def stage_a(blk, slot):
        q0 = pl.multiple_of(blk * qn, qn)
        k0 = pl.multiple_of(kch_ref[blk] * qn, qn)
        cls = cls_ref[blk]
        kcat_scr[0:kn, :] = k_ref[pl.ds(k0, kn), :]
        q = q_ref[pl.ds(q0, qn), :]
        kcat = kcat_scr[...]
        for h, sel in enumerate((lane_lo, jnp.logical_not(lane_lo))):
            qm = jnp.where(sel, q, jnp.zeros_like(q))
            s = _dot_t(kcat, qm)
            s_lat = s[:kn, :] + bias_ref[h, cls]
            s_ctx = s[kn:, :]
            s_scr[slot, h, 0:kn, :] = s_lat
            s_scr[slot, h, kn:, :] = s_ctx
            m = jnp.maximum(jnp.max(s_lat, axis=0, keepdims=True),
                            jnp.max(s_ctx, axis=0, keepdims=True))
            m_scr[slot, h] = jnp.broadcast_to(m, (8, qn))

    def stage_b(blk, slot):
        q0 = pl.multiple_of(blk * qn, qn)
        kch = kch_ref[blk]
        vt = jnp.concatenate([vt_scr[kch + j] for j in range(n_kch)] + [vt_scr[n_vch]], axis=1)
        outs = []
        for h in range(2):
            m = m_scr[slot, h][0:1, :]
            p = jnp.exp2(s_scr[slot, h] - m)
            l = jnp.sum(p, axis=0, keepdims=True)
            o_t = _dot(vt, p.astype(BF16))
            outs.append(o_t * (1.0 / l))
        o_t = jnp.where(sub_lo, outs[0], outs[1])
        o_ref[pl.ds(q0, qn), :] = o_t.T.astype(o_ref.dtype)

    stage_a(0, 0)

    def body(blk, carry):
        slot = lax.rem(blk, 2)
        stage_a(blk, slot)
        stage_b(blk - 1, 1 - slot)
        return carry

    lax.fori_loop(1, n_blk, body, 0)
    stage_b(n_blk - 1, (n_blk - 1) % 2)


def _attn_rows_kernel(kch_ref, cls_ref, q_ref, k_ref, v_ref, kc_ref, vc_ref, bias_ref, *rest,
                      qn, kn, n_blk, aliased):
    o_ref = rest[1] if aliased else rest[0]
    lane = lax.broadcasted_iota(jnp.int32, (1, LANES), 1)
    lane_lo = lane < (LANES // 2)
    kc = kc_ref[...]
    vc = vc_ref[...]

    def body(blk, carry):
        q0 = pl.multiple_of(blk * qn, qn)
        k0 = pl.multiple_of(kch_ref[blk] * qn, qn)
        cls = cls_ref[blk]
        q = q_ref[pl.ds(q0, qn), :]
        kw = k_ref[pl.ds(k0, kn), :]
        vw = v_ref[pl.ds(k0, kn), :]
        outs = []
        for h, sel in enumerate((lane_lo, jnp.logical_not(lane_lo))):
            qm = jnp.where(sel, q, jnp.zeros_like(q))
            s_lat = _dot_t(qm, kw) + bias_ref[h, cls]
            s_ctx = _dot_t(qm, kc)
            m = jnp.maximum(jnp.max(s_lat, axis=-1, keepdims=True),
                            jnp.max(s_ctx, axis=-1, keepdims=True))
            p_lat = jnp.exp2(s_lat - m)
            p_ctx = jnp.exp2(s_ctx - m)
            l = jnp.sum(p_lat, axis=-1, keepdims=True) + jnp.sum(p_ctx, axis=-1, keepdims=True)
            o = _dot(p_lat.astype(BF16), vw) + _dot(p_ctx.astype(BF16), vc)
            outs.append(o * (1.0 / l))
        o_ref[pl.ds(q0, qn), :] = jnp.where(lane_lo, outs[0], outs[1]).astype(o_ref.dtype)
        return carry

    lax.fori_loop(0, n_blk, body, 0)


def _attention_latent(qkv, bias_tab, geo, o_prev, *, variant, hp0, n_hp_call, batch, seq, ctx_len,
                      n_out_rows):
    n_hp = qkv.shape[0] // 3
    qn, kn = geo["rq"] * GRID_W, geo["kr"] * GRID_W
    assert ctx_len == qn
    n_cls = bias_tab.shape[1]
    ctx_blk0 = batch * seq // ctx_len
    rows_variant = variant == "rows"
    in_specs = [
        pl.BlockSpec((None, seq, LANES), lambda hp, b, *_: (hp0 + hp, b, 0)),
        pl.BlockSpec((None, seq, LANES), lambda hp, b, *_: (n_hp + hp0 + hp, b, 0)),
        pl.BlockSpec((None, seq, LANES), lambda hp, b, *_: (2 * n_hp + hp0 + hp, b, 0)),
        pl.BlockSpec((None, ctx_len, LANES), lambda hp, b, *_: (n_hp + hp0 + hp, ctx_blk0 + b, 0)),
        pl.BlockSpec((None, ctx_len, LANES),
                     lambda hp, b, *_: (2 * n_hp + hp0 + hp, ctx_blk0 + b, 0)),
        pl.BlockSpec((2, n_cls) + bias_tab.shape[2:], lambda hp, b, *_: (hp0 + hp, 0, 0, 0)),
    ]
    args = [qkv, qkv, qkv, qkv, qkv, bias_tab]
    aliases = {}
    if o_prev is not None:
        in_specs.append(pl.BlockSpec(memory_space=pl.ANY))
        args.append(o_prev)
        aliases = {2 + len(args) - 1: 0}
    if rows_variant:
        scratch = []
        kern = _attn_rows_kernel
    else:
        scratch = [pltpu.VMEM((seq // qn + 1, LANES, qn), BF16),
                   pltpu.VMEM((kn + ctx_len, LANES), BF16),
                   pltpu.VMEM((2, 2, kn + ctx_len, qn), F32),
                   pltpu.VMEM((2, 2, 8, qn), F32)]
        kern = _attn_lat_kernel
    kern = functools.partial(kern, qn=qn, kn=kn, n_blk=geo["n_blk"], aliased=o_prev is not None)
    grid_spec = pltpu.PrefetchScalarGridSpec(
        num_scalar_prefetch=2,
        grid=(n_hp_call, batch),
        in_specs=in_specs,
        out_specs=pl.BlockSpec((None, seq, LANES), lambda hp, b, *_: (hp0 + hp, b, 0)),
        scratch_shapes=scratch,
    )
    return pl.pallas_call(
        kern,
        grid_spec=grid_spec,
        out_shape=jax.ShapeDtypeStruct((n_hp, n_out_rows, LANES), BF16),
        input_output_aliases=aliases,
        compiler_params=_cparams(("parallel", "parallel")),
        name="attn_" + variant,
    )(jnp.asarray(geo["key_chunk"]), jnp.asarray(geo["cls_of"]), *args)


def _attn_ctx_kernel(q_ref, k_ref, v_ref, o_in_ref, o_ref):
    del o_in_ref
    lane = lax.broadcasted_iota(jnp.int32, (1, LANES), 1)
    lane_lo = lane < (LANES // 2)
    q = q_ref[...]
    k = k_ref[...]
    v = v_ref[...]
    outs = []
    for sel in (lane_lo, jnp.logical_not(lane_lo)):
        qm = jnp.where(sel, q, jnp.zeros_like(q))
        s = _dot_t(qm, k)
        p = jnp.exp2(s - jnp.max(s, axis=-1, keepdims=True))
        l = jnp.sum(p, axis=-1, keepdims=True)
        outs.append(_dot(p.astype(BF16), v) / l)
    o_ref[...] = jnp.where(lane_lo, outs[0], outs[1]).astype(o_ref.dtype)


def _attention_ctx(qkv, o_all, *, batch, seq, ctx_len):
    n_hp = qkv.shape[0] // 3
    blk0 = batch * seq // ctx_len
    return pl.pallas_call(
        _attn_ctx_kernel,
        grid=(n_hp, batch),
        in_specs=[
            pl.BlockSpec((None, ctx_len, LANES), lambda hp, b: (hp, blk0 + b, 0)),
            pl.BlockSpec((None, ctx_len, LANES), lambda hp, b: (n_hp + hp, blk0 + b, 0)),
            pl.BlockSpec((None, ctx_len, LANES), lambda hp, b: (2 * n_hp + hp, blk0 + b, 0)),
            pl.BlockSpec(memory_space=pl.ANY),
        ],
        out_specs=pl.BlockSpec((None, ctx_len, LANES), lambda hp, b: (hp, blk0 + b, 0)),
        out_shape=jax.ShapeDtypeStruct(o_all.shape, o_all.dtype),
        input_output_aliases={3: 0},
        compiler_params=_cparams(("parallel", "parallel")),
        name="attn_ctx",
    )(qkv, qkv, qkv, o_all)


def _chunk_with_halo(prev_ref, cur_ref, next_ref, c, n_chunks, halo, t0, t_lat, seq, ctx_len):
    seq_len = jnp.where(t0 < t_lat, seq, ctx_len)
    lo = c * ROW_CHUNK
    starts = lax.rem(t0 + lo, seq_len) == 0
    ends = lax.rem(t0 + lo + ROW_CHUNK, seq_len) == 0
    prev = prev_ref[...] if c == 0 else cur_ref[lo - halo:lo, :]
    nxt = next_ref[...] if c == n_chunks - 1 else cur_ref[lo + ROW_CHUNK:lo + ROW_CHUNK + halo, :]
    prev = jnp.where(starts, jnp.zeros_like(prev), prev)
    nxt = jnp.where(ends, jnp.zeros_like(nxt), nxt)
    return prev, cur_ref[lo:lo + ROW_CHUNK, :], nxt


CONF_HALO = 16
CONF_ROW_CHUNK = 64
SC_HALO = 8


def _conformer_core(up_ref, uc_ref, un_ref, dw_ref, dwb_ref, lng_ref, lnb_ref, scr, cv, *,
                    c, n_chunks, t0, t_lat, seq, ctx_len):
    n_lg, taps = dw_ref.shape[0], dw_ref.shape[1]
    prev, cur, nxt = _chunk_with_halo(up_ref, uc_ref, un_ref, c, n_chunks, CONF_HALO, t0, t_lat,
                                      seq, ctx_len)
    for g in range(n_lg):
        cols = slice(g * LANES, (g + 1) * LANES)
        scr[c, g, 0:CONF_HALO, :] = prev[:, cols]
        scr[c, g, CONF_HALO:CONF_HALO + ROW_CHUNK, :] = cur[:, cols]
        scr[c, g, CONF_HALO + ROW_CHUNK:, :] = nxt[:, cols]
    base = CONF_HALO - (taps - 1) // 2
    for g in range(n_lg):
        w_g = dw_ref[g]
        b_g = dwb_ref[g]
        for r0 in range(0, ROW_CHUNK, CONF_ROW_CHUNK):
            acc = jnp.broadcast_to(b_g, (CONF_ROW_CHUNK, LANES))
            for k in range(taps):
                acc = acc + w_g[k:k + 1, :] * scr[c, g, r0 + base + k:r0 + base + k + CONF_ROW_CHUNK, :]
            cv[c, g, r0:r0 + CONF_ROW_CHUNK, :] = acc
    u = jnp.concatenate([cv[c, g] for g in range(n_lg)], axis=-1)
    mu = jnp.mean(u, axis=-1, keepdims=True)
    uc = u - mu
    var = jnp.mean(uc * uc, axis=-1, keepdims=True)
    y = uc * lax.rsqrt(var + EPS) * lng_ref[...] + lnb_ref[...]
    return (y * jax.nn.sigmoid(y)).astype(BF16)


def _shortconv_core(gp_ref, gc_ref, gn_ref, bg_ref, cw_ref, scr, *, c, n_chunks, t0, t_lat, seq,
                    ctx_len):
    prev, cur, nxt = _chunk_with_halo(gp_ref, gc_ref, gn_ref, c, n_chunks, SC_HALO, t0, t_lat,
                                      seq, ctx_len)
    scr[c, 0:SC_HALO, :] = prev
    scr[c, SC_HALO:SC_HALO + ROW_CHUNK, :] = cur
    scr[c, SC_HALO + ROW_CHUNK:, :] = nxt
    conv = (cw_ref[0:1, :] * scr[c, SC_HALO - 1:SC_HALO - 1 + ROW_CHUNK, :]
            + cw_ref[1:2, :] * scr[c, SC_HALO:SC_HALO + ROW_CHUNK, :]
            + cw_ref[2:3, :] * scr[c, SC_HALO + 1:SC_HALO + 1 + ROW_CHUNK, :])
    lo = c * ROW_CHUNK
    return (bg_ref[lo:lo + ROW_CHUNK, :] * conv).astype(BF16)


def _post_mixer_kernel(*refs, variant, n_in, n_scr, has_bias, final, tm, t_lat, seq, ctx_len):
    mixer_refs = refs[:n_in]
    x_ref, mod_ref, g2_ref, wout_ref, w1_ref, w2_ref = refs[n_in:n_in + 6]
    rest = list(refs[n_in + 6:])
    bout_ref = rest.pop(0) if has_bias else None
    fg_ref = rest.pop(0) if final else None
    o_ref = rest.pop(0)
    scr_refs = rest
    assert len(scr_refs) == n_scr

    def tail(a, rows):
        y = _dot(a, wout_ref[...])
        if has_bias:
            y = y + bout_ref[...]
        x1 = x_ref[rows, :] + mod_ref[2:3, :] * y
        m = _modulate(_rmsnorm(x1, g2_ref[...]), mod_ref, 3).astype(BF16)
        h = jnp.maximum(_dot(m, w1_ref[...]), 0.0)
        h = (h * h).astype(BF16)
        x2 = x1 + mod_ref[5:6, :] * _dot(h, w2_ref[...])
        if final:
            x2 = _rmsnorm(x2, fg_ref[...])
        o_ref[rows, :] = x2

    n_chunks = tm // ROW_CHUNK
    t0 = pl.program_id(0) * tm
    for c in range(n_chunks):
        rows = slice(c * ROW_CHUNK, (c + 1) * ROW_CHUNK)
        if variant == "plain":
            a_ref = mixer_refs[0]
            a = jnp.concatenate([a_ref[j, rows, :] for j in range(a_ref.shape[0])], axis=1)
        else:
            core = _conformer_core if variant == "conformer" else _shortconv_core
            a = core(*mixer_refs, *scr_refs, c=c, n_chunks=n_chunks, t0=t0, t_lat=t_lat, seq=seq,
                     ctx_len=ctx_len)
        tail(a, rows)


def _post_mixer(variant, mixer_args, mixer_specs, scratch, x, mod, g2, wout, bout, w1, w2,
                final_g, *, layer, n_rows, mod_row, tm, t_lat, seq, ctx_len):
    d = x.shape[1]
    f = w1.shape[1]
    in_specs = list(mixer_specs) + [
        pl.BlockSpec((tm, d), lambda i: (i, 0)),
        pl.BlockSpec((None, None, N_MOD, d), lambda i: (layer, mod_row(i), 0, 0)),
        pl.BlockSpec((None, 1, d), lambda i: (layer, 0, 0)),
        _resident((d, d)),
        _resident((d, f)),
        _resident((f, d)),
    ]
    args = list(mixer_args) + [x, mod, g2, wout, w1, w2]
    if bout is not None:
        in_specs.append(_resident((1, d)))
        args.append(bout)
    if final_g is not None:
        in_specs.append(_resident((1, d)))
        args.append(final_g)
    kern = functools.partial(_post_mixer_kernel, variant=variant, n_in=len(mixer_args),
                             n_scr=len(scratch), has_bias=bout is not None,
                             final=final_g is not None, tm=tm, t_lat=t_lat, seq=seq,
                             ctx_len=ctx_len)
    return pl.pallas_call(
        kern,
        grid=(n_rows // tm,),
        in_specs=in_specs,
        out_specs=pl.BlockSpec((tm, d), lambda i: (i, 0)),
        out_shape=jax.ShapeDtypeStruct((n_rows, d), F32),
        scratch_shapes=list(scratch),
        compiler_params=_cparams(("parallel",)),
        name="post_mixer_" + variant,
    )(*args)


def _halo_specs(n_total_rows, tm, halo, d):
    hb = tm // halo
    n_hblk = n_total_rows // halo
    return [
        pl.BlockSpec((halo, d), lambda i: (jnp.maximum(i * hb - 1, 0), 0)),
        pl.BlockSpec((tm, d), lambda i: (i, 0)),
        pl.BlockSpec((halo, d), lambda i: (jnp.minimum((i + 1) * hb, n_hblk - 1), 0)),
    ]


def kernel(x, c, ctx, c_ctx, mod_w, mod_b, norm1_g, norm2_g, mlp_w1, mlp_w2, na_wqkv, na_wo,
           na_rpb, cv_w1, cv_b1, cv_dw, cv_dwb, cv_ln_g, cv_ln_b, cv_w2, cv_b2, sc_win, sc_conv,
           sc_wout, final_g):
    batch, seq, d = x.shape
    ctx_len = ctx.shape[1]
    depth = mod_w.shape[0]
    t_lat = batch * seq
    t_all = t_lat + batch * ctx_len
    head_dim = d // N_HEADS
    assert 2 * head_dim == LANES and batch < MOD_ROWS and seq % GRID_W == 0
    rows = seq // GRID_W
    tm = 2 * ROW_CHUNK
    n_chunks = tm // ROW_CHUNK
    assert seq % tm == 0 and (batch * ctx_len) % tm == 0
    assert seq % ROW_CHUNK == 0 and ctx_len % ROW_CHUNK == 0
    last_attn = max(i for i in range(depth) if i % N_MIXERS == 0)
    n_lg = d // LANES

    def mod_row_fn(tile):
        def f(i):
            return jnp.where(i * tile < t_lat, (i * tile) // seq, batch)
        return f

    cc = jnp.concatenate([c, c_ctx[None], jnp.zeros((MOD_ROWS - batch - 1, d), F32)], axis=0)
    mod = _modulation(cc, mod_w, mod_b)

    tok = jnp.concatenate([x.reshape(t_lat, d), ctx.reshape(batch * ctx_len, d)], axis=0)
    g1 = norm1_g.reshape(depth, 1, d)
    g2 = norm2_g.reshape(depth, 1, d)
    geo = _attn_geometry(rows)

    for i in range(depth):
        kind = i % N_MIXERS
        slot = i // N_MIXERS
        ctx_live = i < last_attn
        n_rows = t_all if ctx_live else t_lat
        inp = functools.partial(_in_proj, tok, mod, g1, layer=i, mod_row=mod_row_fn(tm), tm=tm)
        post = functools.partial(
            _post_mixer, x=tok, mod=mod, g2=g2, w1=mlp_w1[i].astype(BF16),
            w2=mlp_w2[i].astype(BF16), final_g=final_g[None] if i == depth - 1 else None,
            layer=i, n_rows=n_rows, t_lat=t_lat, seq=seq, ctx_len=ctx_len)
        if kind == 0:
            (qkv,) = inp(na_wqkv[slot].astype(BF16), None, n_rows=t_all,
                         write_out=_write_qkv(d, head_dim ** -0.5 * LOG2E), name="in_proj_qkv",
                         **_lane_group_tiles(t_all, 3 * n_lg, BF16, tm))
            bias_tab = _attn_bias_table(na_rpb[slot], geo)
            half = n_lg // 2
            o = None
            for vi, variant in enumerate(("rows", "pipe")):
                tab = bias_tab.transpose(0, 1, 3, 2) if variant == "rows" else bias_tab
                o = _attention_latent(qkv, tab, geo, o, variant=variant, hp0=vi * half,
                                      n_hp_call=half, batch=batch, seq=seq, ctx_len=ctx_len,
                                      n_out_rows=n_rows)
            if ctx_live:
                o = _attention_ctx(qkv, o, batch=batch, seq=seq, ctx_len=ctx_len)
            tok = post("plain", [o], [pl.BlockSpec((n_lg, tm, LANES), lambda r: (0, r, 0))], [],
                       wout=na_wo[slot].astype(BF16), bout=None, mod_row=mod_row_fn(tm), tm=tm)
        elif kind == 1:
            (u,) = inp(cv_w1[slot].astype(BF16), cv_b1[slot][None], n_rows=n_rows,
                       write_out=_write_glu(d), name="in_proj_glu",
                       **_row_tiles(n_rows, [d], [F32], tm))
            taps = cv_dw.shape[1]
            dw = cv_dw[slot].reshape(taps, n_lg, LANES).transpose(1, 0, 2)
            args = [u, u, u, dw, cv_dwb[slot].reshape(n_lg, 1, LANES), cv_ln_g[slot][None],
                    cv_ln_b[slot][None]]
            specs = _halo_specs(n_rows, tm, CONF_HALO, d) + [
                _resident((n_lg, taps, LANES)), _resident((n_lg, 1, LANES)),
                _resident((1, d)), _resident((1, d))]
            scratch = [pltpu.VMEM((n_chunks, n_lg, ROW_CHUNK + 2 * CONF_HALO, LANES), F32),
                       pltpu.VMEM((n_chunks, n_lg, ROW_CHUNK, LANES), F32)]
            tok = post("conformer", args, specs, scratch, wout=cv_w2[slot].astype(BF16),
                       bout=cv_b2[slot][None], mod_row=mod_row_fn(tm), tm=tm)
        else:
            bg, g = inp(sc_win[slot].astype(BF16), None, n_rows=n_rows,
                        write_out=_write_gated(d), name="in_proj_gated",
                        **_row_tiles(n_rows, [d, d], [F32, F32], tm))
            args = [g, g, g, bg, sc_conv[slot]]
            specs = _halo_specs(n_rows, tm, SC_HALO, d) + [
                pl.BlockSpec((tm, d), lambda r: (r, 0)), _resident(sc_conv[slot].shape)]
            scratch = [pltpu.VMEM((n_chunks, ROW_CHUNK + 2 * SC_HALO, d), F32)]
            tok = post("shortconv", args, specs, scratch, wout=sc_wout[slot].astype(BF16),
                       bout=None, mod_row=mod_row_fn(tm), tm=tm)
    return tok[:t_lat].reshape(batch, seq, d)
```

```python
import functools

import numpy as np
import jax
import jax.numpy as jnp
from jax import lax
from jax.experimental import pallas as pl
from jax.experimental.pallas import tpu as pltpu

N_HEADS = 16
GRID_W = 64
WIN_ROWS_MAX = 8
WIN_COLS = 16
N_MIXERS = 3
N_MOD = 6
EPS = 1e-6
MASK_BIAS = -1e30
LOG2E = 1.4426950408889634

LANES = 128
MOD_ROWS = 16
ATTN_Q_ROWS = 4
ATTN_K_ROWS = 12
ROW_CHUNK = 256
VMEM_LIMIT = 56 * 1024 * 1024

F32 = jnp.float32
BF16 = jnp.bfloat16


def _cparams(sem):
    return pltpu.CompilerParams(dimension_semantics=sem, vmem_limit_bytes=VMEM_LIMIT)


def _resident(shape):
    zeros = (0,) * len(shape)
    return pl.BlockSpec(shape, lambda *_: zeros, pipeline_mode=pl.Buffered(1))


def _rmsnorm(x, g):
    return x * lax.rsqrt(jnp.mean(x * x, axis=-1, keepdims=True) + EPS) * g


def _modulate(y, mod_ref, off):
    return y * (1.0 + mod_ref[off + 1:off + 2, :]) + mod_ref[off:off + 1, :]


def _dot(a, b):
    return jnp.dot(a, b, preferred_element_type=F32)


def _dot_t(a, b):
    return lax.dot_general(a, b, (((1,), (1,)), ((), ())), preferred_element_type=F32)


def _tok_specs(tok, tm):
    if not isinstance(tok, tuple):
        return [pl.BlockSpec((tm, tok.shape[1]), lambda i: (i, 0))], [tok], 0
    lat, ctx = tok
    n_lat = lat.shape[0] // tm
    d = lat.shape[1]
    return ([pl.BlockSpec((tm, d), lambda i: (jnp.minimum(i, n_lat - 1), 0)),
             pl.BlockSpec((tm, d), lambda i: (jnp.maximum(i - n_lat, 0), 0))], [lat, ctx], n_lat)


def _tok_rows(x_refs, rows, n_lat_tiles):
    if len(x_refs) == 1:
        return x_refs[0][rows, :]
    return jnp.where(pl.program_id(0) < n_lat_tiles, x_refs[0][rows, :], x_refs[1][rows, :])


def _mod_kernel(cc_ref, w_ref, b_ref, o_ref):
    s = cc_ref[...]
    s = s * jax.nn.sigmoid(s)
    o_ref[...] = _dot(s.astype(BF16), w_ref[...].astype(BF16)) + b_ref[...]


def _modulation(cc, mod_w, mod_b):
    depth, d, n = mod_w.shape
    tn = n // 4
    out = pl.pallas_call(
        _mod_kernel,
        grid=(depth, n // tn),
        in_specs=[
            pl.BlockSpec((MOD_ROWS, d), lambda l, j: (0, 0)),
            pl.BlockSpec((None, d, tn), lambda l, j: (l, 0, j)),
            pl.BlockSpec((None, 1, tn), lambda l, j: (l, 0, j)),
        ],
        out_specs=pl.BlockSpec((None, MOD_ROWS, tn), lambda l, j: (l, 0, j)),
        out_shape=jax.ShapeDtypeStruct((depth, MOD_ROWS, n), F32),
        compiler_params=_cparams(("parallel", "parallel")),
        name="modulation",
    )(cc, mod_w, mod_b.reshape(depth, 1, n))
    return out.reshape(depth, MOD_ROWS, N_MOD, d)


def _in_proj_kernel(*refs, n_x, n_lat_tiles, has_bias, write_out, tm):
    x_refs = refs[:n_x]
    mod_ref, g_ref, w_ref = refs[n_x:n_x + 3]
    rest = refs[n_x + 3:]
    b_ref = rest[0] if has_bias else None
    out_refs = rest[1:] if has_bias else rest
    for c in range(tm // ROW_CHUNK):
        rows = slice(c * ROW_CHUNK, (c + 1) * ROW_CHUNK)
        x = _tok_rows(x_refs, rows, n_lat_tiles)
        a = _modulate(_rmsnorm(x, g_ref[...]), mod_ref, 0).astype(BF16)
        acc = _dot(a, w_ref[...])
        if has_bias:
            acc = acc + b_ref[...]
        write_out(acc, out_refs, rows)


def _in_proj(x, mod, gain, w, bias, *, layer, n_rows, mod_row, tm, out_specs, out_shape,
             write_out, name):
    d, n = w.shape
    x_specs, x_args, n_lat_tiles = _tok_specs(x, tm)
    in_specs = x_specs + [
        pl.BlockSpec((None, None, N_MOD, d), lambda i: (layer, mod_row(i), 0, 0)),
        pl.BlockSpec((None, 1, d), lambda i: (layer, 0, 0)),
        _resident((d, n)),
    ]
    args = x_args + [mod, gain, w]
    if bias is not None:
        in_specs.append(_resident((1, n)))
        args.append(bias)
    kern = functools.partial(_in_proj_kernel, n_x=len(x_args), n_lat_tiles=n_lat_tiles,
                             has_bias=bias is not None, write_out=write_out, tm=tm)
    return pl.pallas_call(
        kern,
        grid=(n_rows // tm,),
        in_specs=in_specs,
        out_specs=out_specs,
        out_shape=out_shape,
        compiler_params=_cparams(("parallel",)),
        name=name,
    )(*args)


def _row_tiles(n_rows, widths, dtypes, tm):
    return dict(out_specs=[pl.BlockSpec((tm, wd), lambda i: (i, 0)) for wd in widths],
                out_shape=[jax.ShapeDtypeStruct((n_rows, wd), dt)
                           for wd, dt in zip(widths, dtypes)])


def _lane_group_tiles(n_rows, n_groups, dtype, tm):
    return dict(out_specs=[pl.BlockSpec((n_groups, tm, LANES), lambda i: (0, i, 0))],
                out_shape=[jax.ShapeDtypeStruct((n_groups, n_rows, LANES), dtype)])


def _write_qkv(d, scale):
    def write(acc, out_refs, rows):
        (o_ref,) = out_refs
        for j in range(o_ref.shape[0]):
            blk = acc[:, j * LANES:(j + 1) * LANES]
            if j * LANES < d:
                blk = blk * scale
            o_ref[j, rows, :] = blk.astype(o_ref.dtype)
    return write


def _write_glu(d):
    def write(acc, out_refs, rows):
        (o_ref,) = out_refs
        o_ref[rows, :] = acc[:, :d] * jax.nn.sigmoid(acc[:, d:])
    return write


def _write_gated(d):
    def write(acc, out_refs, rows):
        bg_ref, g_ref = out_refs
        bg_ref[rows, :] = acc[:, :d]
        g_ref[rows, :] = acc[:, d:2 * d] * acc[:, 2 * d:]
    return write


def _attn_geometry(rows):
    kh = min(WIN_ROWS_MAX, rows)
    rq, kr = ATTN_Q_ROWS, ATTN_K_ROWS
    assert rows % rq == 0 and rows >= kr and kr % rq == 0
    n_blk = rows // rq
    r_ar = np.arange(rows)
    row_start = np.clip(r_ar - kh // 2, 0, rows - kh)
    c_ar = np.arange(GRID_W)
    col_start = np.clip(c_ar - WIN_COLS // 2, 0, GRID_W - WIN_COLS)
    key_chunk = np.zeros(n_blk, np.int32)
    cls_of = np.zeros(n_blk, np.int32)
    patterns = []
    for blk in range(n_blk):
        r = np.arange(blk * rq, (blk + 1) * rq)
        kb = int(np.clip(row_start[r[0]] // rq * rq, 0, rows - kr))
        assert row_start[r].min() >= kb and row_start[r].max() + kh <= kb + kr
        key_chunk[blk] = kb // rq
        pat = (tuple(r - kb), tuple(row_start[r] - kb))
        if pat not in patterns:
            patterns.append(pat)
        cls_of[blk] = patterns.index(pat)
    n_rpb_rows = 2 * WIN_ROWS_MAX - 1
    row_idx = np.full((len(patterns), rq, kr), n_rpb_rows, np.int32)
    for ci, (r_rel, rs_rel) in enumerate(patterns):
        for qi in range(rq):
            for ki in range(kr):
                if rs_rel[qi] <= ki < rs_rel[qi] + kh:
                    row_idx[ci, qi, ki] = ki - r_rel[qi] + WIN_ROWS_MAX - 1
    col_ok = ((c_ar[None, :] >= col_start[:, None])
              & (c_ar[None, :] < col_start[:, None] + WIN_COLS))
    col_off = c_ar[None, :] - c_ar[:, None] + WIN_COLS - 1
    onehot = np.zeros((2 * WIN_COLS - 1, GRID_W, GRID_W), np.float32)
    c_i, kc_i = np.nonzero(col_ok)
    onehot[col_off[c_i, kc_i], c_i, kc_i] = 1.0
    return dict(rq=rq, kr=kr, n_blk=n_blk, key_chunk=key_chunk, cls_of=cls_of,
                row_idx=row_idx, col_ok=col_ok, onehot=onehot)


def _bias_table_kernel(toe_ref, o_ref, *, row_idx):
    n_cls, rq, kr = row_idx.shape
    for ci in range(n_cls):
        for qi in range(rq):
            o_ref[ci, qi * GRID_W:(qi + 1) * GRID_W, :] = jnp.concatenate(
                [toe_ref[int(row_idx[ci, qi, ki])] for ki in range(kr)], axis=1)


def _attn_bias_table(rpb, geo):
    h = rpb.shape[0]
    toe = jnp.einsum("hrj,jck->hrck", rpb, jnp.asarray(geo["onehot"]),
                     precision=lax.Precision.HIGHEST)
    toe = jnp.where(geo["col_ok"][None, None], toe * LOG2E, MASK_BIAS)
    ext = jnp.concatenate([toe, jnp.full((h, 1, GRID_W, GRID_W), MASK_BIAS, F32)], axis=1)
    n_cls, rq, kr = geo["row_idx"].shape
    return pl.pallas_call(
        functools.partial(_bias_table_kernel, row_idx=geo["row_idx"]),
        grid=(h,),
        in_specs=[pl.BlockSpec((None,) + ext.shape[1:], lambda i: (i, 0, 0, 0))],
        out_specs=pl.BlockSpec((None, n_cls, rq * GRID_W, kr * GRID_W), lambda i: (i, 0, 0, 0)),
        out_shape=jax.ShapeDtypeStruct((h, n_cls, rq * GRID_W, kr * GRID_W), F32),
        compiler_params=_cparams(("parallel",)),
        name="attn_bias_table",
    )(ext)


def _attn_lat_kernel(kch_ref, cls_ref, q_ref, k_ref, v_ref, kc_ref, vc_ref, bias_ref, o_ref, *,
                     qn, kn, n_blk):
    lane = lax.broadcasted_iota(jnp.int32, (1, LANES), 1)
    lane_lo = lane < (LANES // 2)
    kc = kc_ref[...]
    vc = vc_ref[...]

    def body(blk, carry):
        q0 = pl.multiple_of(blk * qn, qn)
        k0 = pl.multiple_of(kch_ref[blk] * qn, qn)
        cls = cls_ref[blk]
        q = q_ref[pl.ds(q0, qn), :]
        kw = k_ref[pl.ds(k0, kn), :]
        vw = v_ref[pl.ds(k0, kn), :]
        outs = []
        for h, sel in enumerate((lane_lo, jnp.logical_not(lane_lo))):
            qm = jnp.where(sel, q, jnp.zeros_like(q))
            s_lat = _dot_t(qm, kw) + bias_ref[h, cls]
            s_ctx = _dot_t(qm, kc)
            m = jnp.maximum(jnp.max(s_lat, axis=-1, keepdims=True),
                            jnp.max(s_ctx, axis=-1, keepdims=True))
            p_lat = jnp.exp2(s_lat - m)
            p_ctx = jnp.exp2(s_ctx - m)
            l = jnp.sum(p_lat, axis=-1, keepdims=True) + jnp.sum(p_ctx, axis=-1, keepdims=True)
            o = _dot(p_lat.astype(BF16), vw) + _dot(p_ctx.astype(BF16), vc)
            outs.append(o * (1.0 / l))
        o_ref[pl.ds(q0, qn), :] = jnp.where(lane_lo, outs[0], outs[1]).astype(o_ref.dtype)
        return carry

    lax.fori_loop(0, n_blk, body, 0)


def _attention_latent(qkv, bias_tab, geo, *, batch, seq, ctx_len, n_out_rows):
    n_hp = qkv.shape[0] // 3
    qn, kn = geo["rq"] * GRID_W, geo["kr"] * GRID_W
    n_cls = bias_tab.shape[1]
    ctx_blk0 = batch * seq // ctx_len
    grid_spec = pltpu.PrefetchScalarGridSpec(
        num_scalar_prefetch=2,
        grid=(n_hp, batch),
        in_specs=[
            pl.BlockSpec((None, seq, LANES), lambda hp, b, *_: (hp, b, 0)),
            pl.BlockSpec((None, seq, LANES), lambda hp, b, *_: (n_hp + hp, b, 0)),
            pl.BlockSpec((None, seq, LANES), lambda hp, b, *_: (2 * n_hp + hp, b, 0)),
            pl.BlockSpec((None, ctx_len, LANES), lambda hp, b, *_: (n_hp + hp, ctx_blk0 + b, 0)),
            pl.BlockSpec((None, ctx_len, LANES), lambda hp, b, *_: (2 * n_hp + hp, ctx_blk0 + b, 0)),
            pl.BlockSpec((2, n_cls, qn, kn), lambda hp, b, *_: (hp, 0, 0, 0)),
        ],
        out_specs=pl.BlockSpec((None, seq, LANES), lambda hp, b, *_: (hp, b, 0)),
    )
    return pl.pallas_call(
        functools.partial(_attn_lat_kernel, qn=qn, kn=kn, n_blk=geo["n_blk"]),
        grid_spec=grid_spec,
        out_shape=jax.ShapeDtypeStruct((n_hp, n_out_rows, LANES), BF16),
        compiler_params=_cparams(("parallel", "parallel")),
        name="attn_latent",
    )(jnp.asarray(geo["key_chunk"]), jnp.asarray(geo["cls_of"]), qkv, qkv, qkv, qkv, qkv, bias_tab)


def _attn_ctx_kernel(q_ref, k_ref, v_ref, o_in_ref, o_ref):
    del o_in_ref
    lane = lax.broadcasted_iota(jnp.int32, (1, LANES), 1)
    lane_lo = lane < (LANES // 2)
    q = q_ref[...]
    k = k_ref[...]
    v = v_ref[...]
    outs = []
    for sel in (lane_lo, jnp.logical_not(lane_lo)):
        qm = jnp.where(sel, q, jnp.zeros_like(q))
        s = _dot_t(qm, k)
        p = jnp.exp2(s - jnp.max(s, axis=-1, keepdims=True))
        l = jnp.sum(p, axis=-1, keepdims=True)
        outs.append(_dot(p.astype(BF16), v) / l)
    o_ref[...] = jnp.where(lane_lo, outs[0], outs[1]).astype(o_ref.dtype)


def _attention_ctx(qkv, o_all, *, batch, seq, ctx_len):
    n_hp = qkv.shape[0] // 3
    blk0 = batch * seq // ctx_len
    return pl.pallas_call(
        _attn_ctx_kernel,
        grid=(n_hp, batch),
        in_specs=[
            pl.BlockSpec((None, ctx_len, LANES), lambda hp, b: (hp, blk0 + b, 0)),
            pl.BlockSpec((None, ctx_len, LANES), lambda hp, b: (n_hp + hp, blk0 + b, 0)),
            pl.BlockSpec((None, ctx_len, LANES), lambda hp, b: (2 * n_hp + hp, blk0 + b, 0)),
            pl.BlockSpec(memory_space=pl.ANY),
        ],
        out_specs=pl.BlockSpec((None, ctx_len, LANES), lambda hp, b: (hp, blk0 + b, 0)),
        out_shape=jax.ShapeDtypeStruct(o_all.shape, o_all.dtype),
        input_output_aliases={3: 0},
        compiler_params=_cparams(("parallel", "parallel")),
        name="attn_ctx",
    )(qkv, qkv, qkv, o_all)


def _chunk_with_halo(prev_ref, cur_ref, next_ref, c, n_chunks, halo, t0, t_lat, seq, ctx_len):
    seq_len = jnp.where(t0 < t_lat, seq, ctx_len)
    lo = c * ROW_CHUNK
    starts = lax.rem(t0 + lo, seq_len) == 0
    ends = lax.rem(t0 + lo + ROW_CHUNK, seq_len) == 0
    prev = prev_ref[...] if c == 0 else cur_ref[lo - halo:lo, :]
    nxt = next_ref[...] if c == n_chunks - 1 else cur_ref[lo + ROW_CHUNK:lo + ROW_CHUNK + halo, :]
    prev = jnp.where(starts, jnp.zeros_like(prev), prev)
    nxt = jnp.where(ends, jnp.zeros_like(nxt), nxt)
    return prev, cur_ref[lo:lo + ROW_CHUNK, :], nxt


CONF_HALO = 16
CONF_ROW_CHUNK = 64
SC_HALO = 8


def _conformer_core(up_ref, uc_ref, un_ref, dw_ref, dwb_ref, lng_ref, lnb_ref, scr, cv, *,
                    c, n_chunks, t0, t_lat, seq, ctx_len):
    n_lg, taps = dw_ref.shape[0], dw_ref.shape[1]
    prev, cur, nxt = _chunk_with_halo(up_ref, uc_ref, un_ref, c, n_chunks, CONF_HALO, t0, t_lat,
                                      seq, ctx_len)
    for g in range(n_lg):
        cols = slice(g * LANES, (g + 1) * LANES)
        scr[c, g, 0:CONF_HALO, :] = prev[:, cols]
        scr[c, g, CONF_HALO:CONF_HALO + ROW_CHUNK, :] = cur[:, cols]
        scr[c, g, CONF_HALO + ROW_CHUNK:, :] = nxt[:, cols]
    base = CONF_HALO - (taps - 1) // 2
    for g in range(n_lg):
        w_g = dw_ref[g]
        b_g = dwb_ref[g]
        for r0 in range(0, ROW_CHUNK, CONF_ROW_CHUNK):
            acc = jnp.broadcast_to(b_g, (CONF_ROW_CHUNK, LANES))
            for k in range(taps):
                acc = acc + w_g[k:k + 1, :] * scr[c, g, r0 + base + k:r0 + base + k + CONF_ROW_CHUNK, :]
            cv[c, g, r0:r0 + CONF_ROW_CHUNK, :] = acc
    u = jnp.concatenate([cv[c, g] for g in range(n_lg)], axis=-1)
    mu = jnp.mean(u, axis=-1, keepdims=True)
    uc = u - mu
    var = jnp.mean(uc * uc, axis=-1, keepdims=True)
    y = uc * lax.rsqrt(var + EPS) * lng_ref[...] + lnb_ref[...]
    return (y * jax.nn.sigmoid(y)).astype(BF16)


def _shortconv_core(gp_ref, gc_ref, gn_ref, bg_ref, cw_ref, scr, *, c, n_chunks, t0, t_lat, seq,
                    ctx_len):
    prev, cur, nxt = _chunk_with_halo(gp_ref, gc_ref, gn_ref, c, n_chunks, SC_HALO, t0, t_lat,
                                      seq, ctx_len)
    scr[c, 0:SC_HALO, :] = prev
    scr[c, SC_HALO:SC_HALO + ROW_CHUNK, :] = cur
    scr[c, SC_HALO + ROW_CHUNK:, :] = nxt
    conv = (cw_ref[0:1, :] * scr[c, SC_HALO - 1:SC_HALO - 1 + ROW_CHUNK, :]
            + cw_ref[1:2, :] * scr[c, SC_HALO:SC_HALO + ROW_CHUNK, :]
            + cw_ref[2:3, :] * scr[c, SC_HALO + 1:SC_HALO + 1 + ROW_CHUNK, :])
    lo = c * ROW_CHUNK
    return (bg_ref[lo:lo + ROW_CHUNK, :] * conv).astype(BF16)


def _post_mixer_kernel(*refs, variant, n_in, n_x, n_lat_tiles, n_scr, has_bias, final, tm, t_lat,
                       seq, ctx_len):
    mixer_refs = refs[:n_in]
    x_refs = refs[n_in:n_in + n_x]
    mod_ref, g2_ref, wout_ref, w1_ref, w2_ref = refs[n_in + n_x:n_in + n_x + 5]
    rest = list(refs[n_in + n_x + 5:])
    bout_ref = rest.pop(0) if has_bias else None
    fg_ref = rest.pop(0) if final else None
    o_ref = rest.pop(0)
    scr_refs = rest
    assert len(scr_refs) == n_scr

    def tail(a, rows):
        y = _dot(a, wout_ref[...])
        if has_bias:
            y = y + bout_ref[...]
        x1 = _tok_rows(x_refs, rows, n_lat_tiles) + mod_ref[2:3, :] * y
        m = _modulate(_rmsnorm(x1, g2_ref[...]), mod_ref, 3).astype(BF16)
        h = jnp.maximum(_dot(m, w1_ref[...]), 0.0)
        h = (h * h).astype(BF16)
        x2 = x1 + mod_ref[5:6, :] * _dot(h, w2_ref[...])
        if final:
            x2 = _rmsnorm(x2, fg_ref[...])
        o_ref[rows, :] = x2

    n_chunks = tm // ROW_CHUNK
    t0 = pl.program_id(0) * tm
    for c in range(n_chunks):
        rows = slice(c * ROW_CHUNK, (c + 1) * ROW_CHUNK)
        if variant == "plain":
            a_ref = mixer_refs[0]
            a = jnp.concatenate([a_ref[j, rows, :] for j in range(a_ref.shape[0])], axis=1)
        else:
            core = _conformer_core if variant == "conformer" else _shortconv_core
            a = core(*mixer_refs, *scr_refs, c=c, n_chunks=n_chunks, t0=t0, t_lat=t_lat, seq=seq,
                     ctx_len=ctx_len)
        tail(a, rows)


def _post_mixer(variant, mixer_args, mixer_specs, scratch, x, mod, g2, wout, bout, w1, w2,
                final_g, *, layer, n_rows, mod_row, tm, t_lat, seq, ctx_len):
    d, f = w1.shape
    x_specs, x_args, n_lat_tiles = _tok_specs(x, tm)
    in_specs = list(mixer_specs) + x_specs + [
        pl.BlockSpec((None, None, N_MOD, d), lambda i: (layer, mod_row(i), 0, 0)),
        pl.BlockSpec((None, 1, d), lambda i: (layer, 0, 0)),
        _resident((d, d)),
        _resident((d, f)),
        _resident((f, d)),
    ]
    args = list(mixer_args) + x_args + [mod, g2, wout, w1, w2]
    if bout is not None:
        in_specs.append(_resident((1, d)))
        args.append(bout)
    if final_g is not None:
        in_specs.append(_resident((1, d)))
        args.append(final_g)
    kern = functools.partial(_post_mixer_kernel, variant=variant, n_in=len(mixer_args),
                             n_x=len(x_args), n_lat_tiles=n_lat_tiles, n_scr=len(scratch),
                             has_bias=bout is not None,
                             final=final_g is not None, tm=tm, t_lat=t_lat, seq=seq,
                             ctx_len=ctx_len)
    return pl.pallas_call(
        kern,
        grid=(n_rows // tm,),
        in_specs=in_specs,
        out_specs=pl.BlockSpec((tm, d), lambda i: (i, 0)),
        out_shape=jax.ShapeDtypeStruct((n_rows, d), F32),
        scratch_shapes=list(scratch),
        compiler_params=_cparams(("parallel",)),
        name="post_mixer_" + variant,
    )(*args)


def _halo_specs(n_total_rows, tm, halo, d):
    hb = tm // halo
    n_hblk = n_total_rows // halo
    return [
        pl.BlockSpec((halo, d), lambda i: (jnp.maximum(i * hb - 1, 0), 0)),
        pl.BlockSpec((tm, d), lambda i: (i, 0)),
        pl.BlockSpec((halo, d), lambda i: (jnp.minimum((i + 1) * hb, n_hblk - 1), 0)),
    ]


def kernel(x, c, ctx, c_ctx, mod_w, mod_b, norm1_g, norm2_g, mlp_w1, mlp_w2, na_wqkv, na_wo,
           na_rpb, cv_w1, cv_b1, cv_dw, cv_dwb, cv_ln_g, cv_ln_b, cv_w2, cv_b2, sc_win, sc_conv,
           sc_wout, final_g):
    batch, seq, d = x.shape
    ctx_len = ctx.shape[1]
    depth = mod_w.shape[0]
    t_lat = batch * seq
    t_all = t_lat + batch * ctx_len
    head_dim = d // N_HEADS
    assert 2 * head_dim == LANES and batch < MOD_ROWS and seq % GRID_W == 0
    rows = seq // GRID_W
    tm = 2 * ROW_CHUNK
    n_chunks = tm // ROW_CHUNK
    assert seq % tm == 0 and (batch * ctx_len) % tm == 0
    assert seq % ROW_CHUNK == 0 and ctx_len % ROW_CHUNK == 0
    last_attn = max(i for i in range(depth) if i % N_MIXERS == 0)
    n_lg = d // LANES

    def mod_row_fn(tile):
        def f(i):
            return jnp.where(i * tile < t_lat, (i * tile) // seq, batch)
        return f

    cc = jnp.concatenate([c, c_ctx[None], jnp.zeros((MOD_ROWS - batch - 1, d), F32)], axis=0)
    mod = _modulation(cc, mod_w, mod_b)

    tok = (x.reshape(t_lat, d), ctx.reshape(batch * ctx_len, d))
    g1 = norm1_g.reshape(depth, 1, d)
    g2 = norm2_g.reshape(depth, 1, d)
    geo = _attn_geometry(rows)

    for i in range(depth):
        kind = i % N_MIXERS
        slot = i // N_MIXERS
        ctx_live = i < last_attn
        n_rows = t_all if ctx_live else t_lat
        inp = functools.partial(_in_proj, tok, mod, g1, layer=i, mod_row=mod_row_fn(tm), tm=tm)
        post = functools.partial(
            _post_mixer, x=tok, mod=mod, g2=g2, w1=mlp_w1[i].astype(BF16),
            w2=mlp_w2[i].astype(BF16), final_g=final_g[None] if i == depth - 1 else None,
            layer=i, n_rows=n_rows, t_lat=t_lat, seq=seq, ctx_len=ctx_len)
        if kind == 0:
            (qkv,) = inp(na_wqkv[slot].astype(BF16), None, n_rows=t_all,
                         write_out=_write_qkv(d, head_dim ** -0.5 * LOG2E), name="in_proj_qkv",
                         **_lane_group_tiles(t_all, 3 * n_lg, BF16, tm))
            bias_tab = _attn_bias_table(na_rpb[slot], geo)
            o = _attention_latent(qkv, bias_tab, geo, batch=batch, seq=seq, ctx_len=ctx_len,
                                  n_out_rows=n_rows)
            if ctx_live:
                o = _attention_ctx(qkv, o, batch=batch, seq=seq, ctx_len=ctx_len)
            tok = post("plain", [o], [pl.BlockSpec((n_lg, tm, LANES), lambda r: (0, r, 0))], [],
                       wout=na_wo[slot].astype(BF16), bout=None, mod_row=mod_row_fn(tm), tm=tm)
        elif kind == 1:
            (u,) = inp(cv_w1[slot].astype(BF16), cv_b1[slot][None], n_rows=n_rows,
                       write_out=_write_glu(d), name="in_proj_glu",
                       **_row_tiles(n_rows, [d], [F32], tm))
            taps = cv_dw.shape[1]
            dw = cv_dw[slot].reshape(taps, n_lg, LANES).transpose(1, 0, 2)
            args = [u, u, u, dw, cv_dwb[slot].reshape(n_lg, 1, LANES), cv_ln_g[slot][None],
                    cv_ln_b[slot][None]]
            specs = _halo_specs(n_rows, tm, CONF_HALO, d) + [
                _resident((n_lg, taps, LANES)), _resident((n_lg, 1, LANES)),
                _resident((1, d)), _resident((1, d))]
            scratch = [pltpu.VMEM((n_chunks, n_lg, ROW_CHUNK + 2 * CONF_HALO, LANES), F32),
                       pltpu.VMEM((n_chunks, n_lg, ROW_CHUNK, LANES), F32)]
            tok = post("conformer", args, specs, scratch, wout=cv_w2[slot].astype(BF16),
                       bout=cv_b2[slot][None], mod_row=mod_row_fn(tm), tm=tm)
        else:
            bg, g = inp(sc_win[slot].astype(BF16), None, n_rows=n_rows,
                        write_out=_write_gated(d), name="in_proj_gated",
                        **_row_tiles(n_rows, [d, d], [F32, F32], tm))
            args = [g, g, g, bg, sc_conv[slot]]
            specs = _halo_specs(n_rows, tm, SC_HALO, d) + [
                pl.BlockSpec((tm, d), lambda r: (r, 0)), _resident(sc_conv[slot].shape)]
            scratch = [pltpu.VMEM((n_chunks, ROW_CHUNK + 2 * SC_HALO, d), F32)]
            tok = post("shortconv", args, specs, scratch, wout=sc_wout[slot].astype(BF16),
                       bout=None, mod_row=mod_row_fn(tm), tm=tm)
    return tok[:t_lat].reshape(batch, seq, d)
```

```python
import functools

import numpy as np
import jax
import jax.numpy as jnp
from jax import lax
from jax.experimental import pallas as pl
from jax.experimental.pallas import tpu as pltpu

N_HEADS = 16
GRID_W = 64
WIN_ROWS_MAX = 8
WIN_COLS = 16
N_MIXERS = 3
N_MOD = 6
EPS = 1e-6
MASK_BIAS = -1e30
LOG2E = 1.4426950408889634

LANES = 128
MOD_ROWS = 16
ATTN_Q_ROWS = 4
ATTN_K_ROWS = 12
ROW_CHUNK = 256
VMEM_LIMIT = 56 * 1024 * 1024

F32 = jnp.float32
BF16 = jnp.bfloat16


def _cparams(sem):
    return pltpu.CompilerParams(dimension_semantics=sem, vmem_limit_bytes=VMEM_LIMIT)


def _resident(shape, layer=None):
    if layer is None:
        zeros = (0,) * len(shape)
        return pl.BlockSpec(shape, lambda *_: zeros, pipeline_mode=pl.Buffered(1))
    index = (layer,) + (0,) * len(shape)
    return pl.BlockSpec((None,) + tuple(shape), lambda *_: index, pipeline_mode=pl.Buffered(1))


def _rmsnorm(x, g):
    return x * lax.rsqrt(jnp.mean(x * x, axis=-1, keepdims=True) + EPS) * g


def _modulate(y, mod_ref, off):
    return y * (1.0 + mod_ref[off + 1:off + 2, :]) + mod_ref[off:off + 1, :]


def _dot(a, b):
    return jnp.dot(a, b, preferred_element_type=F32)


def _dot_t(a, b):
    return lax.dot_general(a, b, (((1,), (1,)), ((), ())), preferred_element_type=F32)


def _tok_specs(tok, tm):
    if not isinstance(tok, tuple):
        return [pl.BlockSpec((tm, tok.shape[1]), lambda i: (i, 0))], [tok], 0
    lat, ctx = tok
    n_lat = lat.shape[0] // tm
    d = lat.shape[1]
    return ([pl.BlockSpec((tm, d), lambda i: (jnp.minimum(i, n_lat - 1), 0)),
             pl.BlockSpec((tm, d), lambda i: (jnp.maximum(i - n_lat, 0), 0))], [lat, ctx], n_lat)


def _tok_rows(x_refs, rows, n_lat_tiles):
    if len(x_refs) == 1:
        return x_refs[0][rows, :]
    return jnp.where(pl.program_id(0) < n_lat_tiles, x_refs[0][rows, :], x_refs[1][rows, :])


def _mod_kernel(cc_ref, w_ref, b_ref, o_ref):
    s = cc_ref[...]
    s = s * jax.nn.sigmoid(s)
    o_ref[...] = _dot(s.astype(BF16), w_ref[...].astype(BF16)) + b_ref[...]


def _modulation(cc, mod_w, mod_b):
    depth, d, n = mod_w.shape
    tn = n // 4
    out = pl.pallas_call(
        _mod_kernel,
        grid=(depth, n // tn),
        in_specs=[
            pl.BlockSpec((MOD_ROWS, d), lambda l, j: (0, 0)),
            pl.BlockSpec((None, d, tn), lambda l, j: (l, 0, j)),
            pl.BlockSpec((None, 1, tn), lambda l, j: (l, 0, j)),
        ],
        out_specs=pl.BlockSpec((None, MOD_ROWS, tn), lambda l, j: (l, 0, j)),
        out_shape=jax.ShapeDtypeStruct((depth, MOD_ROWS, n), F32),
        compiler_params=_cparams(("parallel", "parallel")),
        name="modulation",
    )(cc, mod_w, mod_b.reshape(depth, 1, n))
    return out.reshape(depth, MOD_ROWS, N_MOD, d)


def _in_proj_kernel(*refs, n_x, n_lat_tiles, has_bias, write_out, tm):
    x_refs = refs[:n_x]
    mod_ref, g_ref, w_ref = refs[n_x:n_x + 3]
    rest = refs[n_x + 3:]
    b_ref = rest[0] if has_bias else None
    out_refs = rest[1:] if has_bias else rest
    for c in range(tm // ROW_CHUNK):
        rows = slice(c * ROW_CHUNK, (c + 1) * ROW_CHUNK)
        x = _tok_rows(x_refs, rows, n_lat_tiles)
        a = _modulate(_rmsnorm(x, g_ref[...]), mod_ref, 0).astype(BF16)
        acc = _dot(a, w_ref[...])
        if has_bias:
            acc = acc + b_ref[...]
        write_out(acc, out_refs, rows)


def _in_proj(x, mod, gain, w, bias, *, layer, n_rows, mod_row, tm, out_specs, out_shape,
             write_out, name):
    d, n = w.shape
    x_specs, x_args, n_lat_tiles = _tok_specs(x, tm)
    in_specs = x_specs + [
        pl.BlockSpec((None, None, N_MOD, d), lambda i: (layer, mod_row(i), 0, 0)),
        pl.BlockSpec((None, 1, d), lambda i: (layer, 0, 0)),
        _resident((d, n)),
    ]
    args = x_args + [mod, gain, w]
    if bias is not None:
        in_specs.append(_resident((1, n)))
        args.append(bias)
    kern = functools.partial(_in_proj_kernel, n_x=len(x_args), n_lat_tiles=n_lat_tiles,
                             has_bias=bias is not None, write_out=write_out, tm=tm)
    return pl.pallas_call(
        kern,
        grid=(n_rows // tm,),
        in_specs=in_specs,
        out_specs=out_specs,
        out_shape=out_shape,
        compiler_params=_cparams(("parallel",)),
        name=name,
    )(*args)


def _row_tiles(n_rows, widths, dtypes, tm):
    return dict(out_specs=[pl.BlockSpec((tm, wd), lambda i: (i, 0)) for wd in widths],
                out_shape=[jax.ShapeDtypeStruct((n_rows, wd), dt)
                           for wd, dt in zip(widths, dtypes)])


def _lane_group_tiles(n_rows, n_groups, dtype, tm):
    return dict(out_specs=[pl.BlockSpec((n_groups, tm, LANES), lambda i: (0, i, 0))],
                out_shape=[jax.ShapeDtypeStruct((n_groups, n_rows, LANES), dtype)])


def _write_qkv(d, scale):
    def write(acc, out_refs, rows):
        (o_ref,) = out_refs
        for j in range(o_ref.shape[0]):
            blk = acc[:, j * LANES:(j + 1) * LANES]
            if j * LANES < d:
                blk = blk * scale
            o_ref[j, rows, :] = blk.astype(o_ref.dtype)
    return write


def _write_glu(d):
    def write(acc, out_refs, rows):
        (o_ref,) = out_refs
        o_ref[rows, :] = acc[:, :d] * jax.nn.sigmoid(acc[:, d:])
    return write


def _write_gated(d):
    def write(acc, out_refs, rows):
        bg_ref, g_ref = out_refs
        bg_ref[rows, :] = acc[:, :d]
        g_ref[rows, :] = acc[:, d:2 * d] * acc[:, 2 * d:]
    return write


def _attn_geometry(rows):
    kh = min(WIN_ROWS_MAX, rows)
    rq, kr = ATTN_Q_ROWS, ATTN_K_ROWS
    assert rows % rq == 0 and rows >= kr and kr % rq == 0
    n_blk = rows // rq
    r_ar = np.arange(rows)
    row_start = np.clip(r_ar - kh // 2, 0, rows - kh)
    c_ar = np.arange(GRID_W)
    col_start = np.clip(c_ar - WIN_COLS // 2, 0, GRID_W - WIN_COLS)
    key_chunk = np.zeros(n_blk, np.int32)
    cls_of = np.zeros(n_blk, np.int32)
    patterns = []
    for blk in range(n_blk):
        r = np.arange(blk * rq, (blk + 1) * rq)
        kb = int(np.clip(row_start[r[0]] // rq * rq, 0, rows - kr))
        assert row_start[r].min() >= kb and row_start[r].max() + kh <= kb + kr
        key_chunk[blk] = kb // rq
        pat = (tuple(r - kb), tuple(row_start[r] - kb))
        if pat not in patterns:
            patterns.append(pat)
        cls_of[blk] = patterns.index(pat)
    n_rpb_rows = 2 * WIN_ROWS_MAX - 1
    row_idx = np.full((len(patterns), rq, kr), n_rpb_rows, np.int32)
    for ci, (r_rel, rs_rel) in enumerate(patterns):
        for qi in range(rq):
            for ki in range(kr):
                if rs_rel[qi] <= ki < rs_rel[qi] + kh:
                    row_idx[ci, qi, ki] = ki - r_rel[qi] + WIN_ROWS_MAX - 1
    col_ok = ((c_ar[None, :] >= col_start[:, None])
              & (c_ar[None, :] < col_start[:, None] + WIN_COLS))
    col_off = c_ar[None, :] - c_ar[:, None] + WIN_COLS - 1
    onehot = np.zeros((2 * WIN_COLS - 1, GRID_W, GRID_W), np.float32)
    c_i, kc_i = np.nonzero(col_ok)
    onehot[col_off[c_i, kc_i], c_i, kc_i] = 1.0
    return dict(rq=rq, kr=kr, n_blk=n_blk, key_chunk=key_chunk, cls_of=cls_of,
                row_idx=row_idx, col_ok=col_ok, onehot=onehot)


def _bias_table_kernel(toe_ref, o_ref, *, row_idx):
    n_cls, rq, kr = row_idx.shape
    for ci in range(n_cls):
        for qi in range(rq):
            o_ref[ci, qi * GRID_W:(qi + 1) * GRID_W, :] = jnp.concatenate(
                [toe_ref[int(row_idx[ci, qi, ki])] for ki in range(kr)], axis=1)


def _attn_bias_table(rpb, geo):
    h = rpb.shape[0]
    toe = jnp.einsum("hrj,jck->hrck", rpb, jnp.asarray(geo["onehot"]),
                     precision=lax.Precision.HIGHEST)
    toe = jnp.where(geo["col_ok"][None, None], toe * LOG2E, MASK_BIAS)
    ext = jnp.concatenate([toe, jnp.full((h, 1, GRID_W, GRID_W), MASK_BIAS, F32)], axis=1)
    n_cls, rq, kr = geo["row_idx"].shape
    return pl.pallas_call(
        functools.partial(_bias_table_kernel, row_idx=geo["row_idx"]),
        grid=(h,),
        in_specs=[pl.BlockSpec((None,) + ext.shape[1:], lambda i: (i, 0, 0, 0))],
        out_specs=pl.BlockSpec((None, n_cls, rq * GRID_W, kr * GRID_W), lambda i: (i, 0, 0, 0)),
        out_shape=jax.ShapeDtypeStruct((h, n_cls, rq * GRID_W, kr * GRID_W), F32),
        compiler_params=_cparams(("parallel",)),
        name="attn_bias_table",
    )(ext)


def _attn_lat_kernel(kch_ref, cls_ref, q_ref, k_ref, v_ref, kc_ref, vc_ref, bias_ref, o_ref, *,
                     qn, kn, n_blk):
    lane = lax.broadcasted_iota(jnp.int32, (1, LANES), 1)
    lane_lo = lane < (LANES // 2)
    kc = kc_ref[...]
    vc = vc_ref[...]

    def body(blk, carry):
        q0 = pl.multiple_of(blk * qn, qn)
        k0 = pl.multiple_of(kch_ref[blk] * qn, qn)
        cls = cls_ref[blk]
        q = q_ref[pl.ds(q0, qn), :]
        kw = k_ref[pl.ds(k0, kn), :]
        vw = v_ref[pl.ds(k0, kn), :]
        outs = []
        for h, sel in enumerate((lane_lo, jnp.logical_not(lane_lo))):
            qm = jnp.where(sel, q, jnp.zeros_like(q))
            s_lat = _dot_t(qm, kw) + bias_ref[h, cls]
            s_ctx = _dot_t(qm, kc)
            m = jnp.maximum(jnp.max(s_lat, axis=-1, keepdims=True),
                            jnp.max(s_ctx, axis=-1, keepdims=True))
            p_lat = jnp.exp2(s_lat - m)
            p_ctx = jnp.exp2(s_ctx - m)
            l = jnp.sum(p_lat, axis=-1, keepdims=True) + jnp.sum(p_ctx, axis=-1, keepdims=True)
            o = _dot(p_lat.astype(BF16), vw) + _dot(p_ctx.astype(BF16), vc)
            outs.append(o * (1.0 / l))
        o_ref[pl.ds(q0, qn), :] = jnp.where(lane_lo, outs[0], outs[1]).astype(o_ref.dtype)
        return carry

    lax.fori_loop(0, n_blk, body, 0)


def _attention_latent(qkv, bias_tab, geo, *, batch, seq, ctx_len):
    n_hp = qkv.shape[0] // 3
    qn, kn = geo["rq"] * GRID_W, geo["kr"] * GRID_W
    n_cls = bias_tab.shape[1]
    ctx_blk0 = batch * seq // ctx_len
    grid_spec = pltpu.PrefetchScalarGridSpec(
        num_scalar_prefetch=2,
        grid=(n_hp, batch),
        in_specs=[
            pl.BlockSpec((None, seq, LANES), lambda hp, b, *_: (hp, b, 0)),
            pl.BlockSpec((None, seq, LANES), lambda hp, b, *_: (n_hp + hp, b, 0)),
            pl.BlockSpec((None, seq, LANES), lambda hp, b, *_: (2 * n_hp + hp, b, 0)),
            pl.BlockSpec((None, ctx_len, LANES), lambda hp, b, *_: (n_hp + hp, ctx_blk0 + b, 0)),
            pl.BlockSpec((None, ctx_len, LANES), lambda hp, b, *_: (2 * n_hp + hp, ctx_blk0 + b, 0)),
            pl.BlockSpec((2, n_cls, qn, kn), lambda hp, b, *_: (hp, 0, 0, 0)),
        ],
        out_specs=pl.BlockSpec((None, seq, LANES), lambda hp, b, *_: (hp, b, 0)),
    )
    return pl.pallas_call(
        functools.partial(_attn_lat_kernel, qn=qn, kn=kn, n_blk=geo["n_blk"]),
        grid_spec=grid_spec,
        out_shape=jax.ShapeDtypeStruct((n_hp, batch * seq, LANES), BF16),
        compiler_params=_cparams(("parallel", "parallel")),
        name="attn_latent",
    )(jnp.asarray(geo["key_chunk"]), jnp.asarray(geo["cls_of"]), qkv, qkv, qkv, qkv, qkv, bias_tab)


def _attn_ctx_kernel(q_ref, k_ref, v_ref, o_ref):
    lane = lax.broadcasted_iota(jnp.int32, (1, LANES), 1)
    lane_lo = lane < (LANES // 2)
    for hp in range(q_ref.shape[0]):
        q = q_ref[hp]
        k = k_ref[hp]
        v = v_ref[hp]
        outs = []
        for sel in (lane_lo, jnp.logical_not(lane_lo)):
            qm = jnp.where(sel, q, jnp.zeros_like(q))
            s = _dot_t(qm, k)
            p = jnp.exp2(s - jnp.max(s, axis=-1, keepdims=True))
            l = jnp.sum(p, axis=-1, keepdims=True)
            outs.append(_dot(p.astype(BF16), v) * (1.0 / l))
        o_ref[hp] = jnp.where(lane_lo, outs[0], outs[1]).astype(o_ref.dtype)


def _attention_ctx(qkv, *, batch, seq, ctx_len):
    n_hp = qkv.shape[0] // 3
    blk0 = batch * seq // ctx_len
    return pl.pallas_call(
        _attn_ctx_kernel,
        grid=(batch,),
        in_specs=[pl.BlockSpec((n_hp, ctx_len, LANES), lambda b, part=part: (part, blk0 + b, 0))
                  for part in range(3)],
        out_specs=pl.BlockSpec((n_hp, ctx_len, LANES), lambda b: (0, b, 0)),
        out_shape=jax.ShapeDtypeStruct((n_hp, batch * ctx_len, LANES), BF16),
        compiler_params=_cparams(("parallel",)),
        name="attn_ctx",
    )(qkv, qkv, qkv)


def _chunk_with_halo(prev_ref, cur_ref, next_ref, c, n_chunks, halo, t0, t_lat, seq, ctx_len):
    seq_len = jnp.where(t0 < t_lat, seq, ctx_len)
    lo = c * ROW_CHUNK
    starts = lax.rem(t0 + lo, seq_len) == 0
    ends = lax.rem(t0 + lo + ROW_CHUNK, seq_len) == 0
    prev = prev_ref[...] if c == 0 else cur_ref[lo - halo:lo, :]
    nxt = next_ref[...] if c == n_chunks - 1 else cur_ref[lo + ROW_CHUNK:lo + ROW_CHUNK + halo, :]
    prev = jnp.where(starts, jnp.zeros_like(prev), prev)
    nxt = jnp.where(ends, jnp.zeros_like(nxt), nxt)
    return prev, cur_ref[lo:lo + ROW_CHUNK, :], nxt


CONF_HALO = 16
CONF_ROW_CHUNK = 64
SC_HALO = 8


def _conformer_core(up_ref, uc_ref, un_ref, dw_ref, dwb_ref, lng_ref, lnb_ref, scr, cv, *,
                    c, n_chunks, t0, t_lat, seq, ctx_len):
    n_lg, taps = dw_ref.shape[0], dw_ref.shape[1]
    prev, cur, nxt = _chunk_with_halo(up_ref, uc_ref, un_ref, c, n_chunks, CONF_HALO, t0, t_lat,
                                      seq, ctx_len)
    for g in range(n_lg):
        cols = slice(g * LANES, (g + 1) * LANES)
        scr[c, g, 0:CONF_HALO, :] = prev[:, cols]
        scr[c, g, CONF_HALO:CONF_HALO + ROW_CHUNK, :] = cur[:, cols]
        scr[c, g, CONF_HALO + ROW_CHUNK:, :] = nxt[:, cols]
    base = CONF_HALO - (taps - 1) // 2
    for g in range(n_lg):
        w_g = dw_ref[g]
        b_g = dwb_ref[g]
        for r0 in range(0, ROW_CHUNK, CONF_ROW_CHUNK):
            acc = jnp.broadcast_to(b_g, (CONF_ROW_CHUNK, LANES))
            for k in range(taps):
                acc = acc + w_g[k:k + 1, :] * scr[c, g, r0 + base + k:r0 + base + k + CONF_ROW_CHUNK, :]
            cv[c, g, r0:r0 + CONF_ROW_CHUNK, :] = acc
    u = jnp.concatenate([cv[c, g] for g in range(n_lg)], axis=-1)
    mu = jnp.mean(u, axis=-1, keepdims=True)
    uc = u - mu
    var = jnp.mean(uc * uc, axis=-1, keepdims=True)
    y = uc * lax.rsqrt(var + EPS) * lng_ref[...] + lnb_ref[...]
    return (y * jax.nn.sigmoid(y)).astype(BF16)


def _shortconv_core(gp_ref, gc_ref, gn_ref, bg_ref, cw_ref, scr, *, c, n_chunks, t0, t_lat, seq,
                    ctx_len):
    prev, cur, nxt = _chunk_with_halo(gp_ref, gc_ref, gn_ref, c, n_chunks, SC_HALO, t0, t_lat,
                                      seq, ctx_len)
    scr[c, 0:SC_HALO, :] = prev
    scr[c, SC_HALO:SC_HALO + ROW_CHUNK, :] = cur
    scr[c, SC_HALO + ROW_CHUNK:, :] = nxt
    conv = (cw_ref[0:1, :] * scr[c, SC_HALO - 1:SC_HALO - 1 + ROW_CHUNK, :]
            + cw_ref[1:2, :] * scr[c, SC_HALO:SC_HALO + ROW_CHUNK, :]
            + cw_ref[2:3, :] * scr[c, SC_HALO + 1:SC_HALO + 1 + ROW_CHUNK, :])
    lo = c * ROW_CHUNK
    return (bg_ref[lo:lo + ROW_CHUNK, :] * conv).astype(BF16)


def _slab_rows(a_refs, j, rows, n_lat_tiles):
    if len(a_refs) == 1:
        return a_refs[0][j, rows, :]
    return jnp.where(pl.program_id(0) < n_lat_tiles, a_refs[0][j, rows, :], a_refs[1][j, rows, :])


def _post_mixer_kernel(*refs, variant, n_in, n_x, n_lat_tiles, a_lat_tiles, n_scr, has_bias, final,
                       tm, t_lat, seq, ctx_len):
    mixer_refs = refs[:n_in]
    x_refs = refs[n_in:n_in + n_x]
    mod_ref, g2_ref, wout_ref, w1_ref, w2_ref = refs[n_in + n_x:n_in + n_x + 5]
    rest = list(refs[n_in + n_x + 5:])
    bout_ref = rest.pop(0) if has_bias else None
    fg_ref = rest.pop(0) if final else None
    o_ref = rest.pop(0)
    scr_refs = rest
    assert len(scr_refs) == n_scr

    def tail(a, rows):
        y = _dot(a, wout_ref[...])
        if has_bias:
            y = y + bout_ref[...]
        x1 = _tok_rows(x_refs, rows, n_lat_tiles) + mod_ref[2:3, :] * y
        m = _modulate(_rmsnorm(x1, g2_ref[...]), mod_ref, 3).astype(BF16)
        h = jnp.maximum(_dot(m, w1_ref[...]), 0.0)
        h = (h * h).astype(BF16)
        x2 = x1 + mod_ref[5:6, :] * _dot(h, w2_ref[...])
        if final:
            x2 = _rmsnorm(x2, fg_ref[...])
        o_ref[rows, :] = x2

    n_chunks = tm // ROW_CHUNK
    geom = dict(n_chunks=n_chunks, t0=pl.program_id(0) * tm, t_lat=t_lat, seq=seq, ctx_len=ctx_len)
    for c in range(n_chunks):
        rows = slice(c * ROW_CHUNK, (c + 1) * ROW_CHUNK)
        if variant == "plain":
            a = jnp.concatenate([_slab_rows(mixer_refs, j, rows, a_lat_tiles)
                                 for j in range(mixer_refs[0].shape[0])], axis=1)
        else:
            core = _conformer_core if variant == "conformer" else _shortconv_core
            a = core(*mixer_refs, *scr_refs, c=c, **geom)
        tail(a, rows)


def _post_mixer(variant, mixer_args, mixer_specs, scratch, x, mod, g2, wout, bout, w1, w2,
                final_g, *, layer, n_rows, mod_row, tm, t_lat, seq, ctx_len):
    d, f = w1.shape[1:]
    x_specs, x_args, n_lat_tiles = _tok_specs(x, tm)
    in_specs = list(mixer_specs) + x_specs + [
        pl.BlockSpec((None, None, N_MOD, d), lambda i: (layer, mod_row(i), 0, 0)),
        pl.BlockSpec((None, 1, d), lambda i: (layer, 0, 0)),
        _resident((d, d)),
        _resident((d, f), layer),
        _resident((f, d), layer),
    ]
    args = list(mixer_args) + x_args + [mod, g2, wout, w1, w2]
    if bout is not None:
        in_specs.append(_resident((1, d)))
        args.append(bout)
    if final_g is not None:
        in_specs.append(_resident((1, d)))
        args.append(final_g)
    kern = functools.partial(_post_mixer_kernel, variant=variant, n_in=len(mixer_args),
                             n_x=len(x_args), n_lat_tiles=n_lat_tiles,
                             a_lat_tiles=t_lat // tm, n_scr=len(scratch),
                             has_bias=bout is not None,
                             final=final_g is not None, tm=tm, t_lat=t_lat, seq=seq,
                             ctx_len=ctx_len)
    return pl.pallas_call(
        kern,
        grid=(n_rows // tm,),
        in_specs=in_specs,
        out_specs=pl.BlockSpec((tm, d), lambda i: (i, 0)),
        out_shape=jax.ShapeDtypeStruct((n_rows, d), F32),
        scratch_shapes=list(scratch),
        compiler_params=_cparams(("parallel",)),
        name="post_mixer_" + variant,
    )(*args)


def _halo_specs(n_total_rows, tm, halo, d):
    hb = tm // halo
    n_hblk = n_total_rows // halo
    return [
        pl.BlockSpec((halo, d), lambda i: (jnp.maximum(i * hb - 1, 0), 0)),
        pl.BlockSpec((tm, d), lambda i: (i, 0)),
        pl.BlockSpec((halo, d), lambda i: (jnp.minimum((i + 1) * hb, n_hblk - 1), 0)),
    ]


def kernel(x, c, ctx, c_ctx, mod_w, mod_b, norm1_g, norm2_g, mlp_w1, mlp_w2, na_wqkv, na_wo,
           na_rpb, cv_w1, cv_b1, cv_dw, cv_dwb, cv_ln_g, cv_ln_b, cv_w2, cv_b2, sc_win, sc_conv,
           sc_wout, final_g):
    batch, seq, d = x.shape
    ctx_len = ctx.shape[1]
    depth = mod_w.shape[0]
    t_lat = batch * seq
    t_all = t_lat + batch * ctx_len
    head_dim = d // N_HEADS
    assert 2 * head_dim == LANES and batch < MOD_ROWS and seq % GRID_W == 0
    rows = seq // GRID_W
    tm = 2 * ROW_CHUNK
    n_chunks = tm // ROW_CHUNK
    assert seq % tm == 0 and (batch * ctx_len) % tm == 0
    assert seq % ROW_CHUNK == 0 and ctx_len % ROW_CHUNK == 0
    last_attn = max(i for i in range(depth) if i % N_MIXERS == 0)
    n_lg = d // LANES

    def mod_row_fn(tile):
        def f(i):
            return jnp.where(i * tile < t_lat, (i * tile) // seq, batch)
        return f

    cc = jnp.concatenate([c, c_ctx[None], jnp.zeros((MOD_ROWS - batch - 1, d), F32)], axis=0)
    mod = _modulation(cc, mod_w, mod_b)

    tok = (x.reshape(t_lat, d), ctx.reshape(batch * ctx_len, d))
    g1 = norm1_g.reshape(depth, 1, d)
    g2 = norm2_g.reshape(depth, 1, d)
    geo = _attn_geometry(rows)
    w1_all = mlp_w1.astype(BF16)
    w2_all = mlp_w2.astype(BF16)

    for i in range(depth):
        kind = i % N_MIXERS
        slot = i // N_MIXERS
        ctx_live = i < last_attn
        n_rows = t_all if ctx_live else t_lat
        inp = functools.partial(_in_proj, tok, mod, g1, layer=i, mod_row=mod_row_fn(tm), tm=tm)
        post = functools.partial(
            _post_mixer, x=tok, mod=mod, g2=g2, w1=w1_all, w2=w2_all,
            final_g=final_g[None] if i == depth - 1 else None,
            layer=i, n_rows=n_rows, t_lat=t_lat, seq=seq, ctx_len=ctx_len)
        if kind == 0:
            (qkv,) = inp(na_wqkv[slot].astype(BF16), None, n_rows=t_all,
                         write_out=_write_qkv(d, head_dim ** -0.5 * LOG2E), name="in_proj_qkv",
                         **_lane_group_tiles(t_all, 3 * n_lg, BF16, tm))
            bias_tab = _attn_bias_table(na_rpb[slot], geo)
            o = [_attention_latent(qkv, bias_tab, geo, batch=batch, seq=seq, ctx_len=ctx_len)]
            o_specs = [pl.BlockSpec((n_lg, tm, LANES), lambda r: (0, r, 0))]
            if ctx_live:
                o.append(_attention_ctx(qkv, batch=batch, seq=seq, ctx_len=ctx_len))
                n_lat = t_lat // tm
                o_specs = [
                    pl.BlockSpec((n_lg, tm, LANES), lambda r: (0, jnp.minimum(r, n_lat - 1), 0)),
                    pl.BlockSpec((n_lg, tm, LANES), lambda r: (0, jnp.maximum(r - n_lat, 0), 0))]
            tok = post("plain", o, o_specs, [], wout=na_wo[slot].astype(BF16), bout=None,
                       mod_row=mod_row_fn(tm), tm=tm)
        elif kind == 1:
            (u,) = inp(cv_w1[slot].astype(BF16), cv_b1[slot][None], n_rows=n_rows,
                       write_out=_write_glu(d), name="in_proj_glu",
                       **_row_tiles(n_rows, [d], [F32], tm))
            taps = cv_dw.shape[1]
            dw = cv_dw[slot].reshape(taps, n_lg, LANES).transpose(1, 0, 2)
            args = [u, u, u, dw, cv_dwb[slot].reshape(n_lg, 1, LANES), cv_ln_g[slot][None],
                    cv_ln_b[slot][None]]
            specs = _halo_specs(n_rows, tm, CONF_HALO, d) + [
                _resident((n_lg, taps, LANES)), _resident((n_lg, 1, LANES)),
                _resident((1, d)), _resident((1, d))]
            scratch = [pltpu.VMEM((n_chunks, n_lg, ROW_CHUNK + 2 * CONF_HALO, LANES), F32),
                       pltpu.VMEM((n_chunks, n_lg, ROW_CHUNK, LANES), F32)]
            tok = post("conformer", args, specs, scratch, wout=cv_w2[slot].astype(BF16),
                       bout=cv_b2[slot][None], mod_row=mod_row_fn(tm), tm=tm)
        else:
            bg, g = inp(sc_win[slot].astype(BF16), None, n_rows=n_rows,
                        write_out=_write_gated(d), name="in_proj_gated",
                        **_row_tiles(n_rows, [d, d], [F32, F32], tm))
            args = [g, g, g, bg, sc_conv[slot]]
            specs = _halo_specs(n_rows, tm, SC_HALO, d) + [
                pl.BlockSpec((tm, d), lambda r: (r, 0)), _resident(sc_conv[slot].shape)]
            scratch = [pltpu.VMEM((n_chunks, ROW_CHUNK + 2 * SC_HALO, d), F32)]
            tok = post("shortconv", args, specs, scratch, wout=sc_wout[slot].astype(BF16),
                       bout=None, mod_row=mod_row_fn(tm), tm=tm)
    return tok[:t_lat].reshape(batch, seq, d)
```

```python
import functools

import numpy as np
import jax
import jax.numpy as jnp
from jax import lax
from jax.experimental import pallas as pl
from jax.experimental.pallas import tpu as pltpu

N_HEADS = 16
GRID_W = 64
WIN_ROWS_MAX = 8
WIN_COLS = 16
N_MIXERS = 3
N_MOD = 6
EPS = 1e-6
MASK_BIAS = -1e30
LOG2E = 1.4426950408889634

LANES = 128
MOD_ROWS = 16
ATTN_Q_ROWS = 4
ATTN_K_ROWS = 12
ROW_CHUNK = 256
VMEM_LIMIT = 56 * 1024 * 1024

F32 = jnp.float32
BF16 = jnp.bfloat16


def _cparams(sem):
    return pltpu.CompilerParams(dimension_semantics=sem, vmem_limit_bytes=VMEM_LIMIT)


def _resident(shape, layer=None):
    if layer is None:
        zeros = (0,) * len(shape)
        return pl.BlockSpec(shape, lambda *_: zeros, pipeline_mode=pl.Buffered(1))
    index = (layer,) + (0,) * len(shape)
    return pl.BlockSpec((None,) + tuple(shape), lambda *_: index, pipeline_mode=pl.Buffered(1))


def _rmsnorm(x, g):
    return x * lax.rsqrt(jnp.mean(x * x, axis=-1, keepdims=True) + EPS) * g


def _modulate(y, mod_ref, off):
    return y * (1.0 + mod_ref[off + 1:off + 2, :]) + mod_ref[off:off + 1, :]


def _dot(a, b):
    return jnp.dot(a, b, preferred_element_type=F32)


def _dot_t(a, b):
    return lax.dot_general(a, b, (((1,), (1,)), ((), ())), preferred_element_type=F32)


def _tok_specs(tok, tm):
    if not isinstance(tok, tuple):
        return [pl.BlockSpec((tm, tok.shape[1]), lambda i: (i, 0))], [tok], 0
    lat, ctx = tok
    n_lat = lat.shape[0] // tm
    d = lat.shape[1]
    return ([pl.BlockSpec((tm, d), lambda i: (jnp.minimum(i, n_lat - 1), 0)),
             pl.BlockSpec((tm, d), lambda i: (jnp.maximum(i - n_lat, 0), 0))], [lat, ctx], n_lat)


def _tok_rows(x_refs, rows, n_lat_tiles):
    if len(x_refs) == 1:
        return x_refs[0][rows, :]
    return jnp.where(pl.program_id(0) < n_lat_tiles, x_refs[0][rows, :], x_refs[1][rows, :])


def _mod_kernel(cc_ref, w_ref, b_ref, o_ref):
    s = cc_ref[...]
    s = s * jax.nn.sigmoid(s)
    o_ref[...] = _dot(s.astype(BF16), w_ref[...].astype(BF16)) + b_ref[...]


def _modulation(cc, mod_w, mod_b):
    depth, d, n = mod_w.shape
    tn = n // 4
    out = pl.pallas_call(
        _mod_kernel,
        grid=(depth, n // tn),
        in_specs=[
            pl.BlockSpec((MOD_ROWS, d), lambda l, j: (0, 0)),
            pl.BlockSpec((None, d, tn), lambda l, j: (l, 0, j)),
            pl.BlockSpec((None, 1, tn), lambda l, j: (l, 0, j)),
        ],
        out_specs=pl.BlockSpec((None, MOD_ROWS, tn), lambda l, j: (l, 0, j)),
        out_shape=jax.ShapeDtypeStruct((depth, MOD_ROWS, n), F32),
        compiler_params=_cparams(("parallel", "parallel")),
        name="modulation",
    )(cc, mod_w, mod_b.reshape(depth, 1, n))
    return out.reshape(depth, MOD_ROWS, N_MOD, d)


def _in_proj_kernel(*refs, n_x, n_lat_tiles, has_bias, write_out, tm):
    x_refs = refs[:n_x]
    mod_ref, g_ref, w_ref = refs[n_x:n_x + 3]
    rest = refs[n_x + 3:]
    b_ref = rest[0] if has_bias else None
    out_refs = rest[1:] if has_bias else rest
    for c in range(tm // ROW_CHUNK):
        rows = slice(c * ROW_CHUNK, (c + 1) * ROW_CHUNK)
        x = _tok_rows(x_refs, rows, n_lat_tiles)
        a = _modulate(_rmsnorm(x, g_ref[...]), mod_ref, 0).astype(BF16)
        acc = _dot(a, w_ref[...])
        if has_bias:
            acc = acc + b_ref[...]
        write_out(acc, out_refs, rows)


def _in_proj(x, mod, gain, w, bias, *, layer, n_rows, mod_row, tm, out_specs, out_shape,
             write_out, name):
    d, n = w.shape
    x_specs, x_args, n_lat_tiles = _tok_specs(x, tm)
    in_specs = x_specs + [
        pl.BlockSpec((None, None, N_MOD, d), lambda i: (layer, mod_row(i), 0, 0)),
        pl.BlockSpec((None, 1, d), lambda i: (layer, 0, 0)),
        _resident((d, n)),
    ]
    args = x_args + [mod, gain, w]
    if bias is not None:
        in_specs.append(_resident((1, n)))
        args.append(bias)
    kern = functools.partial(_in_proj_kernel, n_x=len(x_args), n_lat_tiles=n_lat_tiles,
                             has_bias=bias is not None, write_out=write_out, tm=tm)
    return pl.pallas_call(
        kern,
        grid=(n_rows // tm,),
        in_specs=in_specs,
        out_specs=out_specs,
        out_shape=out_shape,
        compiler_params=_cparams(("parallel",)),
        name=name,
    )(*args)


def _row_tiles(n_rows, widths, dtypes, tm):
    return dict(out_specs=[pl.BlockSpec((tm, wd), lambda i: (i, 0)) for wd in widths],
                out_shape=[jax.ShapeDtypeStruct((n_rows, wd), dt)
                           for wd, dt in zip(widths, dtypes)])


def _lane_group_tiles(n_rows, n_groups, dtype, tm):
    return dict(out_specs=[pl.BlockSpec((n_groups, tm, LANES), lambda i: (0, i, 0))],
                out_shape=[jax.ShapeDtypeStruct((n_groups, n_rows, LANES), dtype)])


def _write_qkv(d, scale):
    def write(acc, out_refs, rows):
        (o_ref,) = out_refs
        for j in range(o_ref.shape[0]):
            blk = acc[:, j * LANES:(j + 1) * LANES]
            if j * LANES < d:
                blk = blk * scale
            o_ref[j, rows, :] = blk.astype(o_ref.dtype)
    return write


def _write_glu(d):
    def write(acc, out_refs, rows):
        (o_ref,) = out_refs
        o_ref[rows, :] = acc[:, :d] * jax.nn.sigmoid(acc[:, d:])
    return write


def _write_gated(d):
    def write(acc, out_refs, rows):
        bg_ref, g_ref = out_refs
        bg_ref[rows, :] = acc[:, :d]
        g_ref[rows, :] = acc[:, d:2 * d] * acc[:, 2 * d:]
    return write


def _attn_geometry(rows):
    kh = min(WIN_ROWS_MAX, rows)
    rq, kr = ATTN_Q_ROWS, ATTN_K_ROWS
    assert rows % rq == 0 and rows >= kr and kr % rq == 0
    n_blk = rows // rq
    r_ar = np.arange(rows)
    row_start = np.clip(r_ar - kh // 2, 0, rows - kh)
    c_ar = np.arange(GRID_W)
    col_start = np.clip(c_ar - WIN_COLS // 2, 0, GRID_W - WIN_COLS)
    key_chunk = np.zeros(n_blk, np.int32)
    cls_of = np.zeros(n_blk, np.int32)
    patterns = []
    for blk in range(n_blk):
        r = np.arange(blk * rq, (blk + 1) * rq)
        kb = int(np.clip(row_start[r[0]] // rq * rq, 0, rows - kr))
        assert row_start[r].min() >= kb and row_start[r].max() + kh <= kb + kr
        key_chunk[blk] = kb // rq
        pat = (tuple(r - kb), tuple(row_start[r] - kb))
        if pat not in patterns:
            patterns.append(pat)
        cls_of[blk] = patterns.index(pat)
    n_rpb_rows = 2 * WIN_ROWS_MAX - 1
    row_idx = np.full((len(patterns), rq, kr), n_rpb_rows, np.int32)
    for ci, (r_rel, rs_rel) in enumerate(patterns):
        for qi in range(rq):
            for ki in range(kr):
                if rs_rel[qi] <= ki < rs_rel[qi] + kh:
                    row_idx[ci, qi, ki] = ki - r_rel[qi] + WIN_ROWS_MAX - 1
    col_ok = ((c_ar[None, :] >= col_start[:, None])
              & (c_ar[None, :] < col_start[:, None] + WIN_COLS))
    col_off = c_ar[None, :] - c_ar[:, None] + WIN_COLS - 1
    onehot = np.zeros((2 * WIN_COLS - 1, GRID_W, GRID_W), np.float32)
    c_i, kc_i = np.nonzero(col_ok)
    onehot[col_off[c_i, kc_i], c_i, kc_i] = 1.0
    return dict(rq=rq, kr=kr, n_blk=n_blk, key_chunk=key_chunk, cls_of=cls_of,
                row_idx=row_idx, col_ok=col_ok, onehot=onehot)


def _bias_table_kernel(toe_ref, o_ref, *, row_idx):
    n_cls, rq, kr = row_idx.shape
    for ci in range(n_cls):
        for qi in range(rq):
            o_ref[ci, qi * GRID_W:(qi + 1) * GRID_W, :] = jnp.concatenate(
                [toe_ref[int(row_idx[ci, qi, ki])] for ki in range(kr)], axis=1)


def _attn_bias_table(rpb, geo):
    h = rpb.shape[0]
    toe = jnp.einsum("hrj,jck->hrck", rpb, jnp.asarray(geo["onehot"]),
                     precision=lax.Precision.HIGHEST)
    toe = jnp.where(geo["col_ok"][None, None], toe * LOG2E, MASK_BIAS)
    ext = jnp.concatenate([toe, jnp.full((h, 1, GRID_W, GRID_W), MASK_BIAS, F32)], axis=1)
    n_cls, rq, kr = geo["row_idx"].shape
    return pl.pallas_call(
        functools.partial(_bias_table_kernel, row_idx=geo["row_idx"]),
        grid=(h,),
        in_specs=[pl.BlockSpec((None,) + ext.shape[1:], lambda i: (i, 0, 0, 0))],
        out_specs=pl.BlockSpec((None, n_cls, rq * GRID_W, kr * GRID_W), lambda i: (i, 0, 0, 0)),
        out_shape=jax.ShapeDtypeStruct((h, n_cls, rq * GRID_W, kr * GRID_W), F32),
        compiler_params=_cparams(("parallel",)),
        name="attn_bias_table",
    )(ext)


def _attn_lat_kernel(kch_ref, cls_ref, q_ref, k_ref, v_ref, kc_ref, vc_ref, bias_ref, o_ref, *,
                     qn, kn, n_blk):
    lane = lax.broadcasted_iota(jnp.int32, (1, LANES), 1)
    lane_lo = lane < (LANES // 2)
    kc = kc_ref[...]
    vc = vc_ref[...]

    def body(blk, carry):
        q0 = pl.multiple_of(blk * qn, qn)
        k0 = pl.multiple_of(kch_ref[blk] * qn, qn)
        cls = cls_ref[blk]
        q = q_ref[pl.ds(q0, qn), :]
        kw = k_ref[pl.ds(k0, kn), :]
        vw = v_ref[pl.ds(k0, kn), :]
        outs = []
        for h, sel in enumerate((lane_lo, jnp.logical_not(lane_lo))):
            qm = jnp.where(sel, q, jnp.zeros_like(q))
            s_lat = _dot_t(qm, kw) + bias_ref[h, cls]
            s_ctx = _dot_t(qm, kc)
            m = jnp.maximum(jnp.max(s_lat, axis=-1, keepdims=True),
                            jnp.max(s_ctx, axis=-1, keepdims=True))
            p_lat = jnp.exp2(s_lat - m)
            p_ctx = jnp.exp2(s_ctx - m)
            l = jnp.sum(p_lat, axis=-1, keepdims=True) + jnp.sum(p_ctx, axis=-1, keepdims=True)
            o = _dot(p_lat.astype(BF16), vw) + _dot(p_ctx.astype(BF16), vc)
            outs.append(o * (1.0 / l))
        o_ref[pl.ds(q0, qn), :] = jnp.where(lane_lo, outs[0], outs[1]).astype(o_ref.dtype)
        return carry

    lax.fori_loop(0, n_blk, body, 0)


def _attention_latent(qkv, bias_tab, geo, *, batch, seq, ctx_len):
    n_hp = qkv.shape[0] // 3
    qn, kn = geo["rq"] * GRID_W, geo["kr"] * GRID_W
    n_cls = bias_tab.shape[1]
    ctx_blk0 = batch * seq // ctx_len
    grid_spec = pltpu.PrefetchScalarGridSpec(
        num_scalar_prefetch=2,
        grid=(n_hp, batch),
        in_specs=[
            pl.BlockSpec((None, seq, LANES), lambda hp, b, *_: (hp, b, 0)),
            pl.BlockSpec((None, seq, LANES), lambda hp, b, *_: (n_hp + hp, b, 0)),
            pl.BlockSpec((None, seq, LANES), lambda hp, b, *_: (2 * n_hp + hp, b, 0)),
            pl.BlockSpec((None, ctx_len, LANES), lambda hp, b, *_: (n_hp + hp, ctx_blk0 + b, 0)),
            pl.BlockSpec((None, ctx_len, LANES), lambda hp, b, *_: (2 * n_hp + hp, ctx_blk0 + b, 0)),
            pl.BlockSpec((2, n_cls, qn, kn), lambda hp, b, *_: (hp, 0, 0, 0)),
        ],
        out_specs=pl.BlockSpec((None, seq, LANES), lambda hp, b, *_: (hp, b, 0)),
    )
    return pl.pallas_call(
        functools.partial(_attn_lat_kernel, qn=qn, kn=kn, n_blk=geo["n_blk"]),
        grid_spec=grid_spec,
        out_shape=jax.ShapeDtypeStruct((n_hp, batch * seq, LANES), BF16),
        compiler_params=_cparams(("parallel", "parallel")),
        name="attn_latent",
    )(jnp.asarray(geo["key_chunk"]), jnp.asarray(geo["cls_of"]), qkv, qkv, qkv, qkv, qkv, bias_tab)


def _attn_ctx_kernel(q_ref, k_ref, v_ref, o_ref):
    lane = lax.broadcasted_iota(jnp.int32, (1, LANES), 1)
    lane_lo = lane < (LANES // 2)
    for hp in range(q_ref.shape[0]):
        q = q_ref[hp]
        k = k_ref[hp]
        v = v_ref[hp]
        outs = []
        for sel in (lane_lo, jnp.logical_not(lane_lo)):
            qm = jnp.where(sel, q, jnp.zeros_like(q))
            s = _dot_t(qm, k)
            p = jnp.exp2(s - jnp.max(s, axis=-1, keepdims=True))
            l = jnp.sum(p, axis=-1, keepdims=True)
            outs.append(_dot(p.astype(BF16), v) * (1.0 / l))
        o_ref[hp] = jnp.where(lane_lo, outs[0], outs[1]).astype(o_ref.dtype)


def _attention_ctx(qkv, *, batch, seq, ctx_len):
    n_hp = qkv.shape[0] // 3
    blk0 = batch * seq // ctx_len
    return pl.pallas_call(
        _attn_ctx_kernel,
        grid=(batch,),
        in_specs=[pl.BlockSpec((n_hp, ctx_len, LANES), lambda b, part=part: (part, blk0 + b, 0))
                  for part in range(3)],
        out_specs=pl.BlockSpec((n_hp, ctx_len, LANES), lambda b: (0, b, 0)),
        out_shape=jax.ShapeDtypeStruct((n_hp, batch * ctx_len, LANES), BF16),
        compiler_params=_cparams(("parallel",)),
        name="attn_ctx",
    )(qkv, qkv, qkv)


def _chunk_with_halo(prev_ref, cur_ref, next_ref, c, n_chunks, halo, t0, t_lat, seq, ctx_len):
    seq_len = jnp.where(t0 < t_lat, seq, ctx_len)
    lo = c * ROW_CHUNK
    starts = lax.rem(t0 + lo, seq_len) == 0
    ends = lax.rem(t0 + lo + ROW_CHUNK, seq_len) == 0
    prev = prev_ref[...] if c == 0 else cur_ref[lo - halo:lo, :]
    nxt = next_ref[...] if c == n_chunks - 1 else cur_ref[lo + ROW_CHUNK:lo + ROW_CHUNK + halo, :]
    prev = jnp.where(starts, jnp.zeros_like(prev), prev)
    nxt = jnp.where(ends, jnp.zeros_like(nxt), nxt)
    return prev, cur_ref[lo:lo + ROW_CHUNK, :], nxt


CONF_HALO = 16
CONF_ROW_CHUNK = 64
SC_HALO = 8


def _conformer_core(up_ref, uc_ref, un_ref, dw_ref, dwb_ref, lng_ref, lnb_ref, scr, cv, *,
                    c, n_chunks, t0, t_lat, seq, ctx_len):
    n_lg, taps = dw_ref.shape[0], dw_ref.shape[1]
    prev, cur, nxt = _chunk_with_halo(up_ref, uc_ref, un_ref, c, n_chunks, CONF_HALO, t0, t_lat,
                                      seq, ctx_len)
    for g in range(n_lg):
        cols = slice(g * LANES, (g + 1) * LANES)
        scr[c, g, 0:CONF_HALO, :] = prev[:, cols]
        scr[c, g, CONF_HALO:CONF_HALO + ROW_CHUNK, :] = cur[:, cols]
        scr[c, g, CONF_HALO + ROW_CHUNK:, :] = nxt[:, cols]
    base = CONF_HALO - (taps - 1) // 2
    for g in range(n_lg):
        w_g = dw_ref[g]
        b_g = dwb_ref[g]
        for r0 in range(0, ROW_CHUNK, CONF_ROW_CHUNK):
            acc = jnp.broadcast_to(b_g, (CONF_ROW_CHUNK, LANES))
            for k in range(taps):
                acc = acc + w_g[k:k + 1, :] * scr[c, g, r0 + base + k:r0 + base + k + CONF_ROW_CHUNK, :]
            cv[c, g, r0:r0 + CONF_ROW_CHUNK, :] = acc
    u = jnp.concatenate([cv[c, g] for g in range(n_lg)], axis=-1)
    mu = jnp.mean(u, axis=-1, keepdims=True)
    uc = u - mu
    var = jnp.mean(uc * uc, axis=-1, keepdims=True)
    y = uc * lax.rsqrt(var + EPS) * lng_ref[...] + lnb_ref[...]
    return (y * jax.nn.sigmoid(y)).astype(BF16)


def _shortconv_core(gp_ref, gc_ref, gn_ref, bg_ref, cw_ref, scr, *, c, n_chunks, t0, t_lat, seq,
                    ctx_len):
    prev, cur, nxt = _chunk_with_halo(gp_ref, gc_ref, gn_ref, c, n_chunks, SC_HALO, t0, t_lat,
                                      seq, ctx_len)
    scr[c, 0:SC_HALO, :] = prev
    scr[c, SC_HALO:SC_HALO + ROW_CHUNK, :] = cur
    scr[c, SC_HALO + ROW_CHUNK:, :] = nxt
    conv = (cw_ref[0:1, :] * scr[c, SC_HALO - 1:SC_HALO - 1 + ROW_CHUNK, :]
            + cw_ref[1:2, :] * scr[c, SC_HALO:SC_HALO + ROW_CHUNK, :]
            + cw_ref[2:3, :] * scr[c, SC_HALO + 1:SC_HALO + 1 + ROW_CHUNK, :])
    lo = c * ROW_CHUNK
    return (bg_ref[lo:lo + ROW_CHUNK, :] * conv).astype(BF16)


def _slab_rows(a_refs, j, rows, n_lat_tiles):
    if len(a_refs) == 1:
        return a_refs[0][j, rows, :]
    return jnp.where(pl.program_id(0) < n_lat_tiles, a_refs[0][j, rows, :], a_refs[1][j, rows, :])


def _post_mixer_kernel(*refs, variant, n_in, n_x, n_lat_tiles, a_lat_tiles, n_scr, has_bias, final,
                       tm, t_lat, seq, ctx_len):
    mixer_refs = refs[:n_in]
    x_refs = refs[n_in:n_in + n_x]
    mod_ref, g2_ref, wout_ref, w1_ref, w2_ref = refs[n_in + n_x:n_in + n_x + 5]
    rest = list(refs[n_in + n_x + 5:])
    bout_ref = rest.pop(0) if has_bias else None
    fg_ref = rest.pop(0) if final else None
    o_ref = rest.pop(0)
    scr_refs = rest
    assert len(scr_refs) == n_scr

    def tail(a, rows):
        y = _dot(a, wout_ref[...])
        if has_bias:
            y = y + bout_ref[...]
        x1 = _tok_rows(x_refs, rows, n_lat_tiles) + mod_ref[2:3, :] * y
        m = _modulate(_rmsnorm(x1, g2_ref[...]), mod_ref, 3).astype(BF16)
        h = jnp.maximum(_dot(m, w1_ref[...]), 0.0)
        h = (h * h).astype(BF16)
        x2 = x1 + mod_ref[5:6, :] * _dot(h, w2_ref[...])
        if final:
            x2 = _rmsnorm(x2, fg_ref[...])
        o_ref[rows, :] = x2

    n_chunks = tm // ROW_CHUNK
    geom = dict(n_chunks=n_chunks, t0=pl.program_id(0) * tm, t_lat=t_lat, seq=seq, ctx_len=ctx_len)
    for c in range(n_chunks):
        rows = slice(c * ROW_CHUNK, (c + 1) * ROW_CHUNK)
        if variant == "plain":
            a = jnp.concatenate([_slab_rows(mixer_refs, j, rows, a_lat_tiles)
                                 for j in range(mixer_refs[0].shape[0])], axis=1)
        else:
            core = _conformer_core if variant == "conformer" else _shortconv_core
            a = core(*mixer_refs, *scr_refs, c=c, **geom)
        tail(a, rows)


def _post_mixer(variant, mixer_args, mixer_specs, scratch, x, mod, g2, wout, bout, w1, w2,
                final_g, *, layer, n_rows, mod_row, tm, t_lat, seq, ctx_len):
    d, f = w1.shape[1:]
    x_specs, x_args, n_lat_tiles = _tok_specs(x, tm)
    in_specs = list(mixer_specs) + x_specs + [
        pl.BlockSpec((None, None, N_MOD, d), lambda i: (layer, mod_row(i), 0, 0)),
        pl.BlockSpec((None, 1, d), lambda i: (layer, 0, 0)),
        _resident((d, d)),
        _resident((d, f), layer),
        _resident((f, d), layer),
    ]
    args = list(mixer_args) + x_args + [mod, g2, wout, w1, w2]
    if bout is not None:
        in_specs.append(_resident((1, d)))
        args.append(bout)
    if final_g is not None:
        in_specs.append(_resident((1, d)))
        args.append(final_g)
    kern = functools.partial(_post_mixer_kernel, variant=variant, n_in=len(mixer_args),
                             n_x=len(x_args), n_lat_tiles=n_lat_tiles,
                             a_lat_tiles=t_lat // tm, n_scr=len(scratch),
                             has_bias=bout is not None,
                             final=final_g is not None, tm=tm, t_lat=t_lat, seq=seq,
                             ctx_len=ctx_len)
    return pl.pallas_call(
        kern,
        grid=(n_rows // tm,),
        in_specs=in_specs,
        out_specs=pl.BlockSpec((tm, d), lambda i: (i, 0)),
        out_shape=jax.ShapeDtypeStruct((n_rows, d), F32),
        scratch_shapes=list(scratch),
        compiler_params=_cparams(("parallel",)),
        name="post_mixer_" + variant,
    )(*args)


def _halo_specs(n_total_rows, tm, halo, d):
    hb = tm // halo
    n_hblk = n_total_rows // halo
    return [
        pl.BlockSpec((halo, d), lambda i: (jnp.maximum(i * hb - 1, 0), 0)),
        pl.BlockSpec((tm, d), lambda i: (i, 0)),
        pl.BlockSpec((halo, d), lambda i: (jnp.minimum((i + 1) * hb, n_hblk - 1), 0)),
    ]


def kernel(x, c, ctx, c_ctx, mod_w, mod_b, norm1_g, norm2_g, mlp_w1, mlp_w2, na_wqkv, na_wo,
           na_rpb, cv_w1, cv_b1, cv_dw, cv_dwb, cv_ln_g, cv_ln_b, cv_w2, cv_b2, sc_win, sc_conv,
           sc_wout, final_g):
    batch, seq, d = x.shape
    ctx_len = ctx.shape[1]
    depth = mod_w.shape[0]
    t_lat = batch * seq
    t_all = t_lat + batch * ctx_len
    head_dim = d // N_HEADS
    assert 2 * head_dim == LANES and batch < MOD_ROWS and seq % GRID_W == 0
    rows = seq // GRID_W
    tm = 2 * ROW_CHUNK
    tm_in = 4 * ROW_CHUNK
    n_chunks = tm // ROW_CHUNK
    assert seq % tm_in == 0 and (batch * ctx_len) % tm_in == 0 and tm_in % tm == 0
    assert seq % ROW_CHUNK == 0 and ctx_len % ROW_CHUNK == 0
    last_attn = max(i for i in range(depth) if i % N_MIXERS == 0)
    n_lg = d // LANES

    def mod_row_fn(tile):
        def f(i):
            return jnp.where(i * tile < t_lat, (i * tile) // seq, batch)
        return f

    cc = jnp.concatenate([c, c_ctx[None], jnp.zeros((MOD_ROWS - batch - 1, d), F32)], axis=0)
    mod = _modulation(cc, mod_w, mod_b)

    tok = (x.reshape(t_lat, d), ctx.reshape(batch * ctx_len, d))
    g1 = norm1_g.reshape(depth, 1, d)
    g2 = norm2_g.reshape(depth, 1, d)
    geo = _attn_geometry(rows)
    w1_all = mlp_w1.astype(BF16)
    w2_all = mlp_w2.astype(BF16)

    for i in range(depth):
        kind = i % N_MIXERS
        slot = i // N_MIXERS
        ctx_live = i < last_attn
        n_rows = t_all if ctx_live else t_lat
        inp = functools.partial(_in_proj, tok, mod, g1, layer=i, mod_row=mod_row_fn(tm_in),
                                tm=tm_in)
        post = functools.partial(
            _post_mixer, x=tok, mod=mod, g2=g2, w1=w1_all, w2=w2_all,
            final_g=final_g[None] if i == depth - 1 else None,
            layer=i, n_rows=n_rows, t_lat=t_lat, seq=seq, ctx_len=ctx_len)
        if kind == 0:
            (qkv,) = inp(na_wqkv[slot].astype(BF16), None, n_rows=t_all,
                         write_out=_write_qkv(d, head_dim ** -0.5 * LOG2E), name="in_proj_qkv",
                         **_lane_group_tiles(t_all, 3 * n_lg, BF16, tm_in))
            bias_tab = _attn_bias_table(na_rpb[slot], geo)
            o = [_attention_latent(qkv, bias_tab, geo, batch=batch, seq=seq, ctx_len=ctx_len)]
            o_specs = [pl.BlockSpec((n_lg, tm, LANES), lambda r: (0, r, 0))]
            if ctx_live:
                o.append(_attention_ctx(qkv, batch=batch, seq=seq, ctx_len=ctx_len))
                n_lat = t_lat // tm
                o_specs = [
                    pl.BlockSpec((n_lg, tm, LANES), lambda r: (0, jnp.minimum(r, n_lat - 1), 0)),
                    pl.BlockSpec((n_lg, tm, LANES), lambda r: (0, jnp.maximum(r - n_lat, 0), 0))]
            tok = post("plain", o, o_specs, [], wout=na_wo[slot].astype(BF16), bout=None,
                       mod_row=mod_row_fn(tm), tm=tm)
        elif kind == 1:
            (u,) = inp(cv_w1[slot].astype(BF16), cv_b1[slot][None], n_rows=n_rows,
                       write_out=_write_glu(d), name="in_proj_glu",
                       **_row_tiles(n_rows, [d], [F32], tm_in))
            taps = cv_dw.shape[1]
            dw = cv_dw[slot].reshape(taps, n_lg, LANES).transpose(1, 0, 2)
            args = [u, u, u, dw, cv_dwb[slot].reshape(n_lg, 1, LANES), cv_ln_g[slot][None],
                    cv_ln_b[slot][None]]
            specs = _halo_specs(n_rows, tm, CONF_HALO, d) + [
                _resident((n_lg, taps, LANES)), _resident((n_lg, 1, LANES)),
                _resident((1, d)), _resident((1, d))]
            scratch = [pltpu.VMEM((n_chunks, n_lg, ROW_CHUNK + 2 * CONF_HALO, LANES), F32),
                       pltpu.VMEM((n_chunks, n_lg, ROW_CHUNK, LANES), F32)]
            tok = post("conformer", args, specs, scratch, wout=cv_w2[slot].astype(BF16),
                       bout=cv_b2[slot][None], mod_row=mod_row_fn(tm), tm=tm)
        else:
            bg, g = inp(sc_win[slot].astype(BF16), None, n_rows=n_rows,
                        write_out=_write_gated(d), name="in_proj_gated",
                        **_row_tiles(n_rows, [d, d], [F32, F32], tm_in))
            args = [g, g, g, bg, sc_conv[slot]]
            specs = _halo_specs(n_rows, tm, SC_HALO, d) + [
                pl.BlockSpec((tm, d), lambda r: (r, 0)), _resident(sc_conv[slot].shape)]
            scratch = [pltpu.VMEM((n_chunks, ROW_CHUNK + 2 * SC_HALO, d), F32)]
            tok = post("shortconv", args, specs, scratch, wout=sc_wout[slot].astype(BF16),
                       bout=None, mod_row=mod_row_fn(tm), tm=tm)
    return tok[:t_lat].reshape(batch, seq, d)
```

```python
import functools

import numpy as np
import jax
import jax.numpy as jnp
from jax import lax
from jax.experimental import pallas as pl
from jax.experimental.pallas import tpu as pltpu

N_HEADS = 16
GRID_W = 64
WIN_ROWS_MAX = 8
WIN_COLS = 16
N_MIXERS = 3
N_MOD = 6
EPS = 1e-6
MASK_BIAS = -1e30
LOG2E = 1.4426950408889634

LANES = 128
MOD_ROWS = 16
ATTN_Q_ROWS = 4
ATTN_K_ROWS = 12
ROW_CHUNK = 256
VMEM_LIMIT = 56 * 1024 * 1024

F32 = jnp.float32
BF16 = jnp.bfloat16


def _cparams(sem):
    return pltpu.CompilerParams(dimension_semantics=sem, vmem_limit_bytes=VMEM_LIMIT)


def _resident(shape, layer=None):
    if layer is None:
        zeros = (0,) * len(shape)
        return pl.BlockSpec(shape, lambda *_: zeros, pipeline_mode=pl.Buffered(1))
    index = (layer,) + (0,) * len(shape)
    return pl.BlockSpec((None,) + tuple(shape), lambda *_: index, pipeline_mode=pl.Buffered(1))


def _rmsnorm(x, g):
    return x * lax.rsqrt(jnp.mean(x * x, axis=-1, keepdims=True) + EPS) * g


def _modulate(y, mod_ref, off):
    return y * (1.0 + mod_ref[off + 1:off + 2, :]) + mod_ref[off:off + 1, :]


def _dot(a, b):
    return jnp.dot(a, b, preferred_element_type=F32)


def _dot_t(a, b):
    return lax.dot_general(a, b, (((1,), (1,)), ((), ())), preferred_element_type=F32)


def _tok_specs(tok, tm):
    if not isinstance(tok, tuple):
        return [pl.BlockSpec((tm, tok.shape[1]), lambda i: (i, 0))], [tok], 0
    lat, ctx = tok
    n_lat = lat.shape[0] // tm
    d = lat.shape[1]
    return ([pl.BlockSpec((tm, d), lambda i: (jnp.minimum(i, n_lat - 1), 0)),
             pl.BlockSpec((tm, d), lambda i: (jnp.maximum(i - n_lat, 0), 0))], [lat, ctx], n_lat)


def _tok_rows(x_refs, rows, n_lat_tiles):
    if len(x_refs) == 1:
        return x_refs[0][rows, :]
    return jnp.where(pl.program_id(0) < n_lat_tiles, x_refs[0][rows, :], x_refs[1][rows, :])


def _mod_kernel(cc_ref, w_ref, b_ref, o_ref):
    s = cc_ref[...]
    s = s * jax.nn.sigmoid(s)
    o_ref[...] = _dot(s.astype(BF16), w_ref[...].astype(BF16)) + b_ref[...]


def _modulation(cc, mod_w, mod_b):
    depth, d, n = mod_w.shape
    tn = n // 4
    out = pl.pallas_call(
        _mod_kernel,
        grid=(depth, n // tn),
        in_specs=[
            pl.BlockSpec((MOD_ROWS, d), lambda l, j: (0, 0)),
            pl.BlockSpec((None, d, tn), lambda l, j: (l, 0, j)),
            pl.BlockSpec((None, 1, tn), lambda l, j: (l, 0, j)),
        ],
        out_specs=pl.BlockSpec((None, MOD_ROWS, tn), lambda l, j: (l, 0, j)),
        out_shape=jax.ShapeDtypeStruct((depth, MOD_ROWS, n), F32),
        compiler_params=_cparams(("parallel", "parallel")),
        name="modulation",
    )(cc, mod_w, mod_b.reshape(depth, 1, n))
    return out.reshape(depth, MOD_ROWS, N_MOD, d)


def _in_proj_kernel(*refs, n_x, n_lat_tiles, has_bias, write_out, tm):
    x_refs = refs[:n_x]
    mod_ref, g_ref, w_ref = refs[n_x:n_x + 3]
    rest = refs[n_x + 3:]
    b_ref = rest[0] if has_bias else None
    out_refs = rest[1:] if has_bias else rest
    for c in range(tm // ROW_CHUNK):
        rows = slice(c * ROW_CHUNK, (c + 1) * ROW_CHUNK)
        x = _tok_rows(x_refs, rows, n_lat_tiles)
        a = _modulate(_rmsnorm(x, g_ref[...]), mod_ref, 0).astype(BF16)
        acc = _dot(a, w_ref[...])
        if has_bias:
            acc = acc + b_ref[...]
        write_out(acc, out_refs, rows)


def _in_proj(x, mod, gain, w, bias, *, layer, n_rows, mod_row, tm, out_specs, out_shape,
             write_out, name):
    d, n = w.shape
    x_specs, x_args, n_lat_tiles = _tok_specs(x, tm)
    in_specs = x_specs + [
        pl.BlockSpec((None, None, N_MOD, d), lambda i: (layer, mod_row(i), 0, 0)),
        pl.BlockSpec((None, 1, d), lambda i: (layer, 0, 0)),
        _resident((d, n)),
    ]
    args = x_args + [mod, gain, w]
    if bias is not None:
        in_specs.append(_resident((1, n)))
        args.append(bias)
    kern = functools.partial(_in_proj_kernel, n_x=len(x_args), n_lat_tiles=n_lat_tiles,
                             has_bias=bias is not None, write_out=write_out, tm=tm)
    return pl.pallas_call(
        kern,
        grid=(n_rows // tm,),
        in_specs=in_specs,
        out_specs=out_specs,
        out_shape=out_shape,
        compiler_params=_cparams(("parallel",)),
        name=name,
    )(*args)


def _row_tiles(n_rows, widths, dtypes, tm):
    return dict(out_specs=[pl.BlockSpec((tm, wd), lambda i: (i, 0)) for wd in widths],
                out_shape=[jax.ShapeDtypeStruct((n_rows, wd), dt)
                           for wd, dt in zip(widths, dtypes)])


def _lane_group_tiles(n_rows, n_groups, dtype, tm):
    return dict(out_specs=[pl.BlockSpec((n_groups, tm, LANES), lambda i: (0, i, 0))],
                out_shape=[jax.ShapeDtypeStruct((n_groups, n_rows, LANES), dtype)])


def _write_qkv(d, scale):
    def write(acc, out_refs, rows):
        (o_ref,) = out_refs
        for j in range(o_ref.shape[0]):
            blk = acc[:, j * LANES:(j + 1) * LANES]
            if j * LANES < d:
                blk = blk * scale
            o_ref[j, rows, :] = blk.astype(o_ref.dtype)
    return write


def _write_glu(d):
    def write(acc, out_refs, rows):
        (o_ref,) = out_refs
        o_ref[rows, :] = acc[:, :d] * jax.nn.sigmoid(acc[:, d:])
    return write


def _write_gated(d):
    def write(acc, out_refs, rows):
        bg_ref, g_ref = out_refs
        bg_ref[rows, :] = acc[:, :d]
        g_ref[rows, :] = acc[:, d:2 * d] * acc[:, 2 * d:]
    return write


def _attn_geometry(rows):
    kh = min(WIN_ROWS_MAX, rows)
    rq, kr = ATTN_Q_ROWS, ATTN_K_ROWS
    assert rows % rq == 0 and rows >= kr and kr % rq == 0
    n_blk = rows // rq
    r_ar = np.arange(rows)
    row_start = np.clip(r_ar - kh // 2, 0, rows - kh)
    c_ar = np.arange(GRID_W)
    col_start = np.clip(c_ar - WIN_COLS // 2, 0, GRID_W - WIN_COLS)
    key_chunk = np.zeros(n_blk, np.int32)
    cls_of = np.zeros(n_blk, np.int32)
    patterns = []
    for blk in range(n_blk):
        r = np.arange(blk * rq, (blk + 1) * rq)
        kb = int(np.clip(row_start[r[0]] // rq * rq, 0, rows - kr))
        assert row_start[r].min() >= kb and row_start[r].max() + kh <= kb + kr
        key_chunk[blk] = kb // rq
        pat = (tuple(r - kb), tuple(row_start[r] - kb))
        if pat not in patterns:
            patterns.append(pat)
        cls_of[blk] = patterns.index(pat)
    n_rpb_rows = 2 * WIN_ROWS_MAX - 1
    row_idx = np.full((len(patterns), rq, kr), n_rpb_rows, np.int32)
    for ci, (r_rel, rs_rel) in enumerate(patterns):
        for qi in range(rq):
            for ki in range(kr):
                if rs_rel[qi] <= ki < rs_rel[qi] + kh:
                    row_idx[ci, qi, ki] = ki - r_rel[qi] + WIN_ROWS_MAX - 1
    col_ok = ((c_ar[None, :] >= col_start[:, None])
              & (c_ar[None, :] < col_start[:, None] + WIN_COLS))
    col_off = c_ar[None, :] - c_ar[:, None] + WIN_COLS - 1
    onehot = np.zeros((2 * WIN_COLS - 1, GRID_W, GRID_W), np.float32)
    c_i, kc_i = np.nonzero(col_ok)
    onehot[col_off[c_i, kc_i], c_i, kc_i] = 1.0
    return dict(rq=rq, kr=kr, n_blk=n_blk, key_chunk=key_chunk, cls_of=cls_of,
                row_idx=row_idx, col_ok=col_ok, onehot=onehot)


def _bias_table_kernel(toe_ref, o_ref, *, row_idx):
    n_cls, rq, kr = row_idx.shape
    for ci in range(n_cls):
        for qi in range(rq):
            o_ref[ci, qi * GRID_W:(qi + 1) * GRID_W, :] = jnp.concatenate(
                [toe_ref[int(row_idx[ci, qi, ki])] for ki in range(kr)], axis=1)


def _attn_bias_table(rpb, geo):
    h = rpb.shape[0]
    toe = jnp.einsum("hrj,jck->hrck", rpb, jnp.asarray(geo["onehot"]),
                     precision=lax.Precision.HIGHEST)
    toe = jnp.where(geo["col_ok"][None, None], toe * LOG2E, MASK_BIAS)
    ext = jnp.concatenate([toe, jnp.full((h, 1, GRID_W, GRID_W), MASK_BIAS, F32)], axis=1)
    n_cls, rq, kr = geo["row_idx"].shape
    return pl.pallas_call(
        functools.partial(_bias_table_kernel, row_idx=geo["row_idx"]),
        grid=(h,),
        in_specs=[pl.BlockSpec((None,) + ext.shape[1:], lambda i: (i, 0, 0, 0))],
        out_specs=pl.BlockSpec((None, n_cls, rq * GRID_W, kr * GRID_W), lambda i: (i, 0, 0, 0)),
        out_shape=jax.ShapeDtypeStruct((h, n_cls, rq * GRID_W, kr * GRID_W), F32),
        compiler_params=_cparams(("parallel",)),
        name="attn_bias_table",
    )(ext)


def _attn_lat_kernel(kch_ref, cls_ref, q_ref, k_ref, v_ref, kc_ref, vc_ref, bias_ref, o_ref, *,
                     qn, kn, n_blk):
    lane = lax.broadcasted_iota(jnp.int32, (1, LANES), 1)
    lane_lo = lane < (LANES // 2)
    kc = kc_ref[...]
    vc = vc_ref[...]

    def body(blk, carry):
        q0 = pl.multiple_of(blk * qn, qn)
        k0 = pl.multiple_of(kch_ref[blk] * qn, qn)
        cls = cls_ref[blk]
        q = q_ref[pl.ds(q0, qn), :]
        kw = k_ref[pl.ds(k0, kn), :]
        vw = v_ref[pl.ds(k0, kn), :]
        outs = []
        for h, sel in enumerate((lane_lo, jnp.logical_not(lane_lo))):
            qm = jnp.where(sel, q, jnp.zeros_like(q))
            s_lat = _dot_t(qm, kw) + bias_ref[h, cls]
            s_ctx = _dot_t(qm, kc)
            m = jnp.maximum(jnp.max(s_lat, axis=-1, keepdims=True),
                            jnp.max(s_ctx, axis=-1, keepdims=True))
            p_lat = jnp.exp2(s_lat - m)
            p_ctx = jnp.exp2(s_ctx - m)
            l = jnp.sum(p_lat, axis=-1, keepdims=True) + jnp.sum(p_ctx, axis=-1, keepdims=True)
            o = _dot(p_lat.astype(BF16), vw) + _dot(p_ctx.astype(BF16), vc)
            outs.append(o * (1.0 / l))
        o_ref[pl.ds(q0, qn), :] = jnp.where(lane_lo, outs[0], outs[1]).astype(o_ref.dtype)
        return carry

    lax.fori_loop(0, n_blk, body, 0, unroll=2)


def _attention_latent(qkv, bias_tab, geo, *, batch, seq, ctx_len):
    n_hp = qkv.shape[0] // 3
    qn, kn = geo["rq"] * GRID_W, geo["kr"] * GRID_W
    n_cls = bias_tab.shape[1]
    ctx_blk0 = batch * seq // ctx_len
    grid_spec = pltpu.PrefetchScalarGridSpec(
        num_scalar_prefetch=2,
        grid=(n_hp, batch),
        in_specs=[
            pl.BlockSpec((None, seq, LANES), lambda hp, b, *_: (hp, b, 0)),
            pl.BlockSpec((None, seq, LANES), lambda hp, b, *_: (n_hp + hp, b, 0)),
            pl.BlockSpec((None, seq, LANES), lambda hp, b, *_: (2 * n_hp + hp, b, 0)),
            pl.BlockSpec((None, ctx_len, LANES), lambda hp, b, *_: (n_hp + hp, ctx_blk0 + b, 0)),
            pl.BlockSpec((None, ctx_len, LANES), lambda hp, b, *_: (2 * n_hp + hp, ctx_blk0 + b, 0)),
            pl.BlockSpec((2, n_cls, qn, kn), lambda hp, b, *_: (hp, 0, 0, 0)),
        ],
        out_specs=pl.BlockSpec((None, seq, LANES), lambda hp, b, *_: (hp, b, 0)),
    )
    return pl.pallas_call(
        functools.partial(_attn_lat_kernel, qn=qn, kn=kn, n_blk=geo["n_blk"]),
        grid_spec=grid_spec,
        out_shape=jax.ShapeDtypeStruct((n_hp, batch * seq, LANES), BF16),
        compiler_params=_cparams(("parallel", "parallel")),
        name="attn_latent",
    )(jnp.asarray(geo["key_chunk"]), jnp.asarray(geo["cls_of"]), qkv, qkv, qkv, qkv, qkv, bias_tab)


def _attn_ctx_kernel(q_ref, k_ref, v_ref, o_ref):
    lane = lax.broadcasted_iota(jnp.int32, (1, LANES), 1)
    lane_lo = lane < (LANES // 2)
    for hp in range(q_ref.shape[0]):
        q = q_ref[hp]
        k = k_ref[hp]
        v = v_ref[hp]
        outs = []
        for sel in (lane_lo, jnp.logical_not(lane_lo)):
            qm = jnp.where(sel, q, jnp.zeros_like(q))
            s = _dot_t(qm, k)
            p = jnp.exp2(s - jnp.max(s, axis=-1, keepdims=True))
            l = jnp.sum(p, axis=-1, keepdims=True)
            outs.append(_dot(p.astype(BF16), v) * (1.0 / l))
        o_ref[hp] = jnp.where(lane_lo, outs[0], outs[1]).astype(o_ref.dtype)


def _attention_ctx(qkv, *, batch, seq, ctx_len):
    n_hp = qkv.shape[0] // 3
    blk0 = batch * seq // ctx_len
    return pl.pallas_call(
        _attn_ctx_kernel,
        grid=(batch,),
        in_specs=[pl.BlockSpec((n_hp, ctx_len, LANES), lambda b, part=part: (part, blk0 + b, 0))
                  for part in range(3)],
        out_specs=pl.BlockSpec((n_hp, ctx_len, LANES), lambda b: (0, b, 0)),
        out_shape=jax.ShapeDtypeStruct((n_hp, batch * ctx_len, LANES), BF16),
        compiler_params=_cparams(("parallel",)),
        name="attn_ctx",
    )(qkv, qkv, qkv)


def _chunk_with_halo(prev_ref, cur_ref, next_ref, c, n_chunks, halo, t0, t_lat, seq, ctx_len):
    seq_len = jnp.where(t0 < t_lat, seq, ctx_len)
    lo = c * ROW_CHUNK
    starts = lax.rem(t0 + lo, seq_len) == 0
    ends = lax.rem(t0 + lo + ROW_CHUNK, seq_len) == 0
    prev = prev_ref[...] if c == 0 else cur_ref[lo - halo:lo, :]
    nxt = next_ref[...] if c == n_chunks - 1 else cur_ref[lo + ROW_CHUNK:lo + ROW_CHUNK + halo, :]
    prev = jnp.where(starts, jnp.zeros_like(prev), prev)
    nxt = jnp.where(ends, jnp.zeros_like(nxt), nxt)
    return prev, cur_ref[lo:lo + ROW_CHUNK, :], nxt


CONF_HALO = 16
CONF_ROW_CHUNK = 64
SC_HALO = 8


def _conformer_core(up_ref, uc_ref, un_ref, dw_ref, dwb_ref, lng_ref, lnb_ref, scr, cv, *,
                    c, n_chunks, t0, t_lat, seq, ctx_len):
    n_lg, taps = dw_ref.shape[0], dw_ref.shape[1]
    prev, cur, nxt = _chunk_with_halo(up_ref, uc_ref, un_ref, c, n_chunks, CONF_HALO, t0, t_lat,
                                      seq, ctx_len)
    for g in range(n_lg):
        cols = slice(g * LANES, (g + 1) * LANES)
        scr[c, g, 0:CONF_HALO, :] = prev[:, cols]
        scr[c, g, CONF_HALO:CONF_HALO + ROW_CHUNK, :] = cur[:, cols]
        scr[c, g, CONF_HALO + ROW_CHUNK:, :] = nxt[:, cols]
    base = CONF_HALO - (taps - 1) // 2
    for g in range(n_lg):
        w_g = dw_ref[g]
        b_g = dwb_ref[g]
        for r0 in range(0, ROW_CHUNK, CONF_ROW_CHUNK):
            acc = jnp.broadcast_to(b_g, (CONF_ROW_CHUNK, LANES))
            for k in range(taps):
                acc = acc + w_g[k:k + 1, :] * scr[c, g, r0 + base + k:r0 + base + k + CONF_ROW_CHUNK, :]
            cv[c, g, r0:r0 + CONF_ROW_CHUNK, :] = acc
    u = jnp.concatenate([cv[c, g] for g in range(n_lg)], axis=-1)
    mu = jnp.mean(u, axis=-1, keepdims=True)
    uc = u - mu
    var = jnp.mean(uc * uc, axis=-1, keepdims=True)
    y = uc * lax.rsqrt(var + EPS) * lng_ref[...] + lnb_ref[...]
    return (y * jax.nn.sigmoid(y)).astype(BF16)


def _shortconv_core(gp_ref, gc_ref, gn_ref, bg_ref, cw_ref, scr, *, c, n_chunks, t0, t_lat, seq,
                    ctx_len):
    prev, cur, nxt = _chunk_with_halo(gp_ref, gc_ref, gn_ref, c, n_chunks, SC_HALO, t0, t_lat,
                                      seq, ctx_len)
    scr[c, 0:SC_HALO, :] = prev
    scr[c, SC_HALO:SC_HALO + ROW_CHUNK, :] = cur
    scr[c, SC_HALO + ROW_CHUNK:, :] = nxt
    conv = (cw_ref[0:1, :] * scr[c, SC_HALO - 1:SC_HALO - 1 + ROW_CHUNK, :]
            + cw_ref[1:2, :] * scr[c, SC_HALO:SC_HALO + ROW_CHUNK, :]
            + cw_ref[2:3, :] * scr[c, SC_HALO + 1:SC_HALO + 1 + ROW_CHUNK, :])
    lo = c * ROW_CHUNK
    return (bg_ref[lo:lo + ROW_CHUNK, :] * conv).astype(BF16)


def _slab_rows(a_refs, j, rows, n_lat_tiles):
    if len(a_refs) == 1:
        return a_refs[0][j, rows, :]
    return jnp.where(pl.program_id(0) < n_lat_tiles, a_refs[0][j, rows, :], a_refs[1][j, rows, :])


def _post_mixer_kernel(*refs, variant, n_in, n_x, n_lat_tiles, a_lat_tiles, n_scr, has_bias, final,
                       tm, t_lat, seq, ctx_len):
    mixer_refs = refs[:n_in]
    x_refs = refs[n_in:n_in + n_x]
    mod_ref, g2_ref, wout_ref, w1_ref, w2_ref = refs[n_in + n_x:n_in + n_x + 5]
    rest = list(refs[n_in + n_x + 5:])
    bout_ref = rest.pop(0) if has_bias else None
    fg_ref = rest.pop(0) if final else None
    o_ref = rest.pop(0)
    scr_refs = rest
    assert len(scr_refs) == n_scr

    def tail(a, rows):
        y = _dot(a, wout_ref[...])
        if has_bias:
            y = y + bout_ref[...]
        x1 = _tok_rows(x_refs, rows, n_lat_tiles) + mod_ref[2:3, :] * y
        m = _modulate(_rmsnorm(x1, g2_ref[...]), mod_ref, 3).astype(BF16)
        h = jnp.maximum(_dot(m, w1_ref[...]), 0.0)
        h = (h * h).astype(BF16)
        x2 = x1 + mod_ref[5:6, :] * _dot(h, w2_ref[...])
        if final:
            x2 = _rmsnorm(x2, fg_ref[...])
        o_ref[rows, :] = x2

    n_chunks = tm // ROW_CHUNK
    geom = dict(n_chunks=n_chunks, t0=pl.program_id(0) * tm, t_lat=t_lat, seq=seq, ctx_len=ctx_len)
    for c in range(n_chunks):
        rows = slice(c * ROW_CHUNK, (c + 1) * ROW_CHUNK)
        if variant == "plain":
            a = jnp.concatenate([_slab_rows(mixer_refs, j, rows, a_lat_tiles)
                                 for j in range(mixer_refs[0].shape[0])], axis=1)
        else:
            core = _conformer_core if variant == "conformer" else _shortconv_core
            a = core(*mixer_refs, *scr_refs, c=c, **geom)
        tail(a, rows)


def _post_mixer(variant, mixer_args, mixer_specs, scratch, x, mod, g2, wout, bout, w1, w2,
                final_g, *, layer, n_rows, mod_row, tm, t_lat, seq, ctx_len):
    d, f = w1.shape[1:]
    x_specs, x_args, n_lat_tiles = _tok_specs(x, tm)
    in_specs = list(mixer_specs) + x_specs + [
        pl.BlockSpec((None, None, N_MOD, d), lambda i: (layer, mod_row(i), 0, 0)),
        pl.BlockSpec((None, 1, d), lambda i: (layer, 0, 0)),
        _resident((d, d)),
        _resident((d, f), layer),
        _resident((f, d), layer),
    ]
    args = list(mixer_args) + x_args + [mod, g2, wout, w1, w2]
    if bout is not None:
        in_specs.append(_resident((1, d)))
        args.append(bout)
    if final_g is not None:
        in_specs.append(_resident((1, d)))
        args.append(final_g)
    kern = functools.partial(_post_mixer_kernel, variant=variant, n_in=len(mixer_args),
                             n_x=len(x_args), n_lat_tiles=n_lat_tiles,
                             a_lat_tiles=t_lat // tm, n_scr=len(scratch),
                             has_bias=bout is not None,
                             final=final_g is not None, tm=tm, t_lat=t_lat, seq=seq,
                             ctx_len=ctx_len)
    return pl.pallas_call(
        kern,
        grid=(n_rows // tm,),
        in_specs=in_specs,
        out_specs=pl.BlockSpec((tm, d), lambda i: (i, 0)),
        out_shape=jax.ShapeDtypeStruct((n_rows, d), F32),
        scratch_shapes=list(scratch),
        compiler_params=_cparams(("parallel",)),
        name="post_mixer_" + variant,
    )(*args)


def _halo_specs(n_total_rows, tm, halo, d):
    hb = tm // halo
    n_hblk = n_total_rows // halo
    return [
        pl.BlockSpec((halo, d), lambda i: (jnp.maximum(i * hb - 1, 0), 0)),
        pl.BlockSpec((tm, d), lambda i: (i, 0)),
        pl.BlockSpec((halo, d), lambda i: (jnp.minimum((i + 1) * hb, n_hblk - 1), 0)),
    ]


def kernel(x, c, ctx, c_ctx, mod_w, mod_b, norm1_g, norm2_g, mlp_w1, mlp_w2, na_wqkv, na_wo,
           na_rpb, cv_w1, cv_b1, cv_dw, cv_dwb, cv_ln_g, cv_ln_b, cv_w2, cv_b2, sc_win, sc_conv,
           sc_wout, final_g):
    batch, seq, d = x.shape
    ctx_len = ctx.shape[1]
    depth = mod_w.shape[0]
    t_lat = batch * seq
    t_all = t_lat + batch * ctx_len
    head_dim = d // N_HEADS
    assert 2 * head_dim == LANES and batch < MOD_ROWS and seq % GRID_W == 0
    rows = seq // GRID_W
    tm = 2 * ROW_CHUNK
    tm_in = 4 * ROW_CHUNK
    n_chunks = tm // ROW_CHUNK
    assert seq % tm_in == 0 and (batch * ctx_len) % tm_in == 0 and tm_in % tm == 0
    assert seq % ROW_CHUNK == 0 and ctx_len % ROW_CHUNK == 0
    last_attn = max(i for i in range(depth) if i % N_MIXERS == 0)
    n_lg = d // LANES

    def mod_row_fn(tile):
        def f(i):
            return jnp.where(i * tile < t_lat, (i * tile) // seq, batch)
        return f

    cc = jnp.concatenate([c, c_ctx[None], jnp.zeros((MOD_ROWS - batch - 1, d), F32)], axis=0)
    mod = _modulation(cc, mod_w, mod_b)

    tok = (x.reshape(t_lat, d), ctx.reshape(batch * ctx_len, d))
    g1 = norm1_g.reshape(depth, 1, d)
    g2 = norm2_g.reshape(depth, 1, d)
    geo = _attn_geometry(rows)
    w1_all = mlp_w1.astype(BF16)
    w2_all = mlp_w2.astype(BF16)

    for i in range(depth):
        kind = i % N_MIXERS
        slot = i // N_MIXERS
        ctx_live = i < last_attn
        n_rows = t_all if ctx_live else t_lat
        inp = functools.partial(_in_proj, tok, mod, g1, layer=i, mod_row=mod_row_fn(tm_in),
                                tm=tm_in)
        post = functools.partial(
            _post_mixer, x=tok, mod=mod, g2=g2, w1=w1_all, w2=w2_all,
            final_g=final_g[None] if i == depth - 1 else None,
            layer=i, n_rows=n_rows, t_lat=t_lat, seq=seq, ctx_len=ctx_len)
        if kind == 0:
            (qkv,) = inp(na_wqkv[slot].astype(BF16), None, n_rows=t_all,
                         write_out=_write_qkv(d, head_dim ** -0.5 * LOG2E), name="in_proj_qkv",
                         **_lane_group_tiles(t_all, 3 * n_lg, BF16, tm_in))
            bias_tab = _attn_bias_table(na_rpb[slot], geo)
            o = [_attention_latent(qkv, bias_tab, geo, batch=batch, seq=seq, ctx_len=ctx_len)]
            o_specs = [pl.BlockSpec((n_lg, tm, LANES), lambda r: (0, r, 0))]
            if ctx_live:
                o.append(_attention_ctx(qkv, batch=batch, seq=seq, ctx_len=ctx_len))
                n_lat = t_lat // tm
                o_specs = [
                    pl.BlockSpec((n_lg, tm, LANES), lambda r: (0, jnp.minimum(r, n_lat - 1), 0)),
                    pl.BlockSpec((n_lg, tm, LANES), lambda r: (0, jnp.maximum(r - n_lat, 0), 0))]
            tok = post("plain", o, o_specs, [], wout=na_wo[slot].astype(BF16), bout=None,
                       mod_row=mod_row_fn(tm), tm=tm)
        elif kind == 1:
            (u,) = inp(cv_w1[slot].astype(BF16), cv_b1[slot][None], n_rows=n_rows,
                       write_out=_write_glu(d), name="in_proj_glu",
                       **_row_tiles(n_rows, [d], [F32], tm_in))
            taps = cv_dw.shape[1]
            dw = cv_dw[slot].reshape(taps, n_lg, LANES).transpose(1, 0, 2)
            args = [u, u, u, dw, cv_dwb[slot].reshape(n_lg, 1, LANES), cv_ln_g[slot][None],
                    cv_ln_b[slot][None]]
            specs = _halo_specs(n_rows, tm, CONF_HALO, d) + [
                _resident((n_lg, taps, LANES)), _resident((n_lg, 1, LANES)),
                _resident((1, d)), _resident((1, d))]
            scratch = [pltpu.VMEM((n_chunks, n_lg, ROW_CHUNK + 2 * CONF_HALO, LANES), F32),
                       pltpu.VMEM((n_chunks, n_lg, ROW_CHUNK, LANES), F32)]
            tok = post("conformer", args, specs, scratch, wout=cv_w2[slot].astype(BF16),
                       bout=cv_b2[slot][None], mod_row=mod_row_fn(tm), tm=tm)
        else:
            bg, g = inp(sc_win[slot].astype(BF16), None, n_rows=n_rows,
                        write_out=_write_gated(d), name="in_proj_gated",
                        **_row_tiles(n_rows, [d, d], [F32, F32], tm_in))
            args = [g, g, g, bg, sc_conv[slot]]
            specs = _halo_specs(n_rows, tm, SC_HALO, d) + [
                pl.BlockSpec((tm, d), lambda r: (r, 0)), _resident(sc_conv[slot].shape)]
            scratch = [pltpu.VMEM((n_chunks, ROW_CHUNK + 2 * SC_HALO, d), F32)]
            tok = post("shortconv", args, specs, scratch, wout=sc_wout[slot].astype(BF16),
                       bout=None, mod_row=mod_row_fn(tm), tm=tm)
    return tok[:t_lat].reshape(batch, seq, d)
```

```python
import functools

import numpy as np
import jax
import jax.numpy as jnp
from jax import lax
from jax.experimental import pallas as pl
from jax.experimental.pallas import tpu as pltpu

N_HEADS = 16
GRID_W = 64
WIN_ROWS_MAX = 8
WIN_COLS = 16
N_MIXERS = 3
N_MOD = 6
EPS = 1e-6
MASK_BIAS = -1e30
LOG2E = 1.4426950408889634

LANES = 128
MOD_ROWS = 16
ATTN_Q_ROWS = 4
ATTN_K_ROWS = 12
ROW_CHUNK = 256
VMEM_LIMIT = 56 * 1024 * 1024

F32 = jnp.float32
BF16 = jnp.bfloat16


def _cparams(sem):
    return pltpu.CompilerParams(dimension_semantics=sem, vmem_limit_bytes=VMEM_LIMIT)


def _resident(shape, layer=None):
    if layer is None:
        zeros = (0,) * len(shape)
        return pl.BlockSpec(shape, lambda *_: zeros, pipeline_mode=pl.Buffered(1))
    index = (layer,) + (0,) * len(shape)
    return pl.BlockSpec((None,) + tuple(shape), lambda *_: index, pipeline_mode=pl.Buffered(1))


def _rmsnorm(x, g):
    return x * lax.rsqrt(jnp.mean(x * x, axis=-1, keepdims=True) + EPS) * g


def _modulate(y, mod_ref, off):
    return y * (1.0 + mod_ref[off + 1:off + 2, :]) + mod_ref[off:off + 1, :]


def _dot(a, b):
    return jnp.dot(a, b, preferred_element_type=F32)


def _dot_t(a, b):
    return lax.dot_general(a, b, (((1,), (1,)), ((), ())), preferred_element_type=F32)


def _tok_specs(tok, tm):
    if not isinstance(tok, tuple):
        return [pl.BlockSpec((tm, tok.shape[1]), lambda i: (i, 0))], [tok], 0
    lat, ctx = tok
    n_lat = lat.shape[0] // tm
    d = lat.shape[1]
    return ([pl.BlockSpec((tm, d), lambda i: (jnp.minimum(i, n_lat - 1), 0)),
             pl.BlockSpec((tm, d), lambda i: (jnp.maximum(i - n_lat, 0), 0))], [lat, ctx], n_lat)


def _tok_rows(x_refs, rows, n_lat_tiles):
    if len(x_refs) == 1:
        return x_refs[0][rows, :]
    return jnp.where(pl.program_id(0) < n_lat_tiles, x_refs[0][rows, :], x_refs[1][rows, :])


def _mod_kernel(cc_ref, w_ref, b_ref, o_ref):
    s = cc_ref[...]
    s = s * jax.nn.sigmoid(s)
    o_ref[...] = _dot(s.astype(BF16), w_ref[...].astype(BF16)) + b_ref[...]


def _modulation(cc, mod_w, mod_b):
    depth, d, n = mod_w.shape
    tn = n // 4
    out = pl.pallas_call(
        _mod_kernel,
        grid=(depth, n // tn),
        in_specs=[
            pl.BlockSpec((MOD_ROWS, d), lambda l, j: (0, 0)),
            pl.BlockSpec((None, d, tn), lambda l, j: (l, 0, j)),
            pl.BlockSpec((None, 1, tn), lambda l, j: (l, 0, j)),
        ],
        out_specs=pl.BlockSpec((None, MOD_ROWS, tn), lambda l, j: (l, 0, j)),
        out_shape=jax.ShapeDtypeStruct((depth, MOD_ROWS, n), F32),
        compiler_params=_cparams(("parallel", "parallel")),
        name="modulation",
    )(cc, mod_w, mod_b.reshape(depth, 1, n))
    return out.reshape(depth, MOD_ROWS, N_MOD, d)


def _in_proj_kernel(*refs, n_x, n_lat_tiles, has_bias, write_out, tm):
    x_refs = refs[:n_x]
    mod_ref, g_ref, w_ref = refs[n_x:n_x + 3]
    rest = refs[n_x + 3:]
    b_ref = rest[0] if has_bias else None
    out_refs = rest[1:] if has_bias else rest
    for c in range(tm // ROW_CHUNK):
        rows = slice(c * ROW_CHUNK, (c + 1) * ROW_CHUNK)
        x = _tok_rows(x_refs, rows, n_lat_tiles)
        a = _modulate(_rmsnorm(x, g_ref[...]), mod_ref, 0).astype(BF16)
        acc = _dot(a, w_ref[...])
        if has_bias:
            acc = acc + b_ref[...]
        write_out(acc, out_refs, rows)


def _in_proj(x, mod, gain, w, bias, *, layer, n_rows, mod_row, tm, out_specs, out_shape,
             write_out, name):
    d, n = w.shape
    x_specs, x_args, n_lat_tiles = _tok_specs(x, tm)
    in_specs = x_specs + [
        pl.BlockSpec((None, None, N_MOD, d), lambda i: (layer, mod_row(i), 0, 0)),
        pl.BlockSpec((None, 1, d), lambda i: (layer, 0, 0)),
        _resident((d, n)),
    ]
    args = x_args + [mod, gain, w]
    if bias is not None:
        in_specs.append(_resident((1, n)))
        args.append(bias)
    kern = functools.partial(_in_proj_kernel, n_x=len(x_args), n_lat_tiles=n_lat_tiles,
                             has_bias=bias is not None, write_out=write_out, tm=tm)
    return pl.pallas_call(
        kern,
        grid=(n_rows // tm,),
        in_specs=in_specs,
        out_specs=out_specs,
        out_shape=out_shape,
        compiler_params=_cparams(("parallel",)),
        name=name,
    )(*args)


def _row_tiles(n_rows, widths, dtypes, tm):
    return dict(out_specs=[pl.BlockSpec((tm, wd), lambda i: (i, 0)) for wd in widths],
                out_shape=[jax.ShapeDtypeStruct((n_rows, wd), dt)
                           for wd, dt in zip(widths, dtypes)])


def _lane_group_tiles(n_rows, n_groups, dtype, tm):
    return dict(out_specs=[pl.BlockSpec((n_groups, tm, LANES), lambda i: (0, i, 0))],
                out_shape=[jax.ShapeDtypeStruct((n_groups, n_rows, LANES), dtype)])


def _write_qkv(d, scale):
    def write(acc, out_refs, rows):
        (o_ref,) = out_refs
        for j in range(o_ref.shape[0]):
            blk = acc[:, j * LANES:(j + 1) * LANES]
            if j * LANES < d:
                blk = blk * scale
            o_ref[j, rows, :] = blk.astype(o_ref.dtype)
    return write


def _write_glu(d):
    def write(acc, out_refs, rows):
        (o_ref,) = out_refs
        o_ref[rows, :] = acc[:, :d] * jax.nn.sigmoid(acc[:, d:])
    return write


def _write_gated(d):
    def write(acc, out_refs, rows):
        bg_ref, g_ref = out_refs
        bg_ref[rows, :] = acc[:, :d]
        g_ref[rows, :] = acc[:, d:2 * d] * acc[:, 2 * d:]
    return write


def _attn_geometry(rows):
    kh = min(WIN_ROWS_MAX, rows)
    rq, kr = ATTN_Q_ROWS, ATTN_K_ROWS
    assert rows % rq == 0 and rows >= kr and kr % rq == 0
    n_blk = rows // rq
    r_ar = np.arange(rows)
    row_start = np.clip(r_ar - kh // 2, 0, rows - kh)
    c_ar = np.arange(GRID_W)
    col_start = np.clip(c_ar - WIN_COLS // 2, 0, GRID_W - WIN_COLS)
    key_chunk = np.zeros(n_blk, np.int32)
    cls_of = np.zeros(n_blk, np.int32)
    patterns = []
    for blk in range(n_blk):
        r = np.arange(blk * rq, (blk + 1) * rq)
        kb = int(np.clip(row_start[r[0]] // rq * rq, 0, rows - kr))
        assert row_start[r].min() >= kb and row_start[r].max() + kh <= kb + kr
        key_chunk[blk] = kb // rq
        pat = (tuple(r - kb), tuple(row_start[r] - kb))
        if pat not in patterns:
            patterns.append(pat)
        cls_of[blk] = patterns.index(pat)
    n_rpb_rows = 2 * WIN_ROWS_MAX - 1
    row_idx = np.full((len(patterns), rq, kr), n_rpb_rows, np.int32)
    for ci, (r_rel, rs_rel) in enumerate(patterns):
        for qi in range(rq):
            for ki in range(kr):
                if rs_rel[qi] <= ki < rs_rel[qi] + kh:
                    row_idx[ci, qi, ki] = ki - r_rel[qi] + WIN_ROWS_MAX - 1
    col_ok = ((c_ar[None, :] >= col_start[:, None])
              & (c_ar[None, :] < col_start[:, None] + WIN_COLS))
    col_off = c_ar[None, :] - c_ar[:, None] + WIN_COLS - 1
    onehot = np.zeros((2 * WIN_COLS - 1, GRID_W, GRID_W), np.float32)
    c_i, kc_i = np.nonzero(col_ok)
    onehot[col_off[c_i, kc_i], c_i, kc_i] = 1.0
    return dict(rq=rq, kr=kr, n_blk=n_blk, key_chunk=key_chunk, cls_of=cls_of,
                row_idx=row_idx, col_ok=col_ok, onehot=onehot)


def _bias_table_kernel(toe_ref, o_ref, *, row_idx):
    n_cls, rq, kr = row_idx.shape
    for ci in range(n_cls):
        for qi in range(rq):
            o_ref[ci, qi * GRID_W:(qi + 1) * GRID_W, :] = jnp.concatenate(
                [toe_ref[int(row_idx[ci, qi, ki])] for ki in range(kr)], axis=1)


def _attn_bias_table(rpb, geo):
    h = rpb.shape[0]
    toe = jnp.einsum("hrj,jck->hrck", rpb, jnp.asarray(geo["onehot"]),
                     precision=lax.Precision.HIGHEST)
    toe = jnp.where(geo["col_ok"][None, None], toe * LOG2E, MASK_BIAS)
    ext = jnp.concatenate([toe, jnp.full((h, 1, GRID_W, GRID_W), MASK_BIAS, F32)], axis=1)
    n_cls, rq, kr = geo["row_idx"].shape
    return pl.pallas_call(
        functools.partial(_bias_table_kernel, row_idx=geo["row_idx"]),
        grid=(h,),
        in_specs=[pl.BlockSpec((None,) + ext.shape[1:], lambda i: (i, 0, 0, 0))],
        out_specs=pl.BlockSpec((None, n_cls, rq * GRID_W, kr * GRID_W), lambda i: (i, 0, 0, 0)),
        out_shape=jax.ShapeDtypeStruct((h, n_cls, rq * GRID_W, kr * GRID_W), F32),
        compiler_params=_cparams(("parallel",)),
        name="attn_bias_table",
    )(ext)


def _attn_lat_kernel(kch_ref, cls_ref, q_ref, k_ref, v_ref, kc_ref, vc_ref, bias_ref, o_ref, *,
                     qn, kn, n_blk):
    lane = lax.broadcasted_iota(jnp.int32, (1, LANES), 1)
    lane_lo = lane < (LANES // 2)
    kc = kc_ref[...]
    vc = vc_ref[...]

    def body(blk, carry):
        q0 = pl.multiple_of(blk * qn, qn)
        k0 = pl.multiple_of(kch_ref[blk] * qn, qn)
        cls = cls_ref[blk]
        q = q_ref[pl.ds(q0, qn), :]
        kw = k_ref[pl.ds(k0, kn), :]
        vw = v_ref[pl.ds(k0, kn), :]
        outs = []
        for h, sel in enumerate((lane_lo, jnp.logical_not(lane_lo))):
            qm = jnp.where(sel, q, jnp.zeros_like(q))
            s_lat = _dot_t(qm, kw) + bias_ref[h, cls]
            s_ctx = _dot_t(qm, kc)
            m = jnp.maximum(jnp.max(s_lat, axis=-1, keepdims=True),
                            jnp.max(s_ctx, axis=-1, keepdims=True))
            p_lat = jnp.exp2(s_lat - m)
            p_ctx = jnp.exp2(s_ctx - m)
            l = jnp.sum(p_lat, axis=-1, keepdims=True) + jnp.sum(p_ctx, axis=-1, keepdims=True)
            o = _dot(p_lat.astype(BF16), vw) + _dot(p_ctx.astype(BF16), vc)
            outs.append(o * (1.0 / l))
        o_ref[pl.ds(q0, qn), :] = jnp.where(lane_lo, outs[0], outs[1]).astype(o_ref.dtype)
        return carry

    lax.fori_loop(0, n_blk, body, 0, unroll=4)


def _attention_latent(qkv, bias_tab, geo, *, batch, seq, ctx_len):
    n_hp = qkv.shape[0] // 3
    qn, kn = geo["rq"] * GRID_W, geo["kr"] * GRID_W
    n_cls = bias_tab.shape[1]
    ctx_blk0 = batch * seq // ctx_len
    grid_spec = pltpu.PrefetchScalarGridSpec(
        num_scalar_prefetch=2,
        grid=(n_hp, batch),
        in_specs=[
            pl.BlockSpec((None, seq, LANES), lambda hp, b, *_: (hp, b, 0)),
            pl.BlockSpec((None, seq, LANES), lambda hp, b, *_: (n_hp + hp, b, 0)),
            pl.BlockSpec((None, seq, LANES), lambda hp, b, *_: (2 * n_hp + hp, b, 0)),
            pl.BlockSpec((None, ctx_len, LANES), lambda hp, b, *_: (n_hp + hp, ctx_blk0 + b, 0)),
            pl.BlockSpec((None, ctx_len, LANES), lambda hp, b, *_: (2 * n_hp + hp, ctx_blk0 + b, 0)),
            pl.BlockSpec((2, n_cls, qn, kn), lambda hp, b, *_: (hp, 0, 0, 0)),
        ],
        out_specs=pl.BlockSpec((None, seq, LANES), lambda hp, b, *_: (hp, b, 0)),
    )
    return pl.pallas_call(
        functools.partial(_attn_lat_kernel, qn=qn, kn=kn, n_blk=geo["n_blk"]),
        grid_spec=grid_spec,
        out_shape=jax.ShapeDtypeStruct((n_hp, batch * seq, LANES), BF16),
        compiler_params=_cparams(("parallel", "parallel")),
        name="attn_latent",
    )(jnp.asarray(geo["key_chunk"]), jnp.asarray(geo["cls_of"]), qkv, qkv, qkv, qkv, qkv, bias_tab)


def _attn_ctx_kernel(q_ref, k_ref, v_ref, o_ref):
    lane = lax.broadcasted_iota(jnp.int32, (1, LANES), 1)
    lane_lo = lane < (LANES // 2)
    for hp in range(q_ref.shape[0]):
        q = q_ref[hp]
        k = k_ref[hp]
        v = v_ref[hp]
        outs = []
        for sel in (lane_lo, jnp.logical_not(lane_lo)):
            qm = jnp.where(sel, q, jnp.zeros_like(q))
            s = _dot_t(qm, k)
            p = jnp.exp2(s - jnp.max(s, axis=-1, keepdims=True))
            l = jnp.sum(p, axis=-1, keepdims=True)
            outs.append(_dot(p.astype(BF16), v) * (1.0 / l))
        o_ref[hp] = jnp.where(lane_lo, outs[0], outs[1]).astype(o_ref.dtype)


def _attention_ctx(qkv, *, batch, seq, ctx_len):
    n_hp = qkv.shape[0] // 3
    blk0 = batch * seq // ctx_len
    return pl.pallas_call(
        _attn_ctx_kernel,
        grid=(batch,),
        in_specs=[pl.BlockSpec((n_hp, ctx_len, LANES), lambda b, part=part: (part, blk0 + b, 0))
                  for part in range(3)],
        out_specs=pl.BlockSpec((n_hp, ctx_len, LANES), lambda b: (0, b, 0)),
        out_shape=jax.ShapeDtypeStruct((n_hp, batch * ctx_len, LANES), BF16),
        compiler_params=_cparams(("parallel",)),
        name="attn_ctx",
    )(qkv, qkv, qkv)


def _chunk_with_halo(prev_ref, cur_ref, next_ref, c, n_chunks, halo, t0, t_lat, seq, ctx_len):
    seq_len = jnp.where(t0 < t_lat, seq, ctx_len)
    lo = c * ROW_CHUNK
    starts = lax.rem(t0 + lo, seq_len) == 0
    ends = lax.rem(t0 + lo + ROW_CHUNK, seq_len) == 0
    prev = prev_ref[...] if c == 0 else cur_ref[lo - halo:lo, :]
    nxt = next_ref[...] if c == n_chunks - 1 else cur_ref[lo + ROW_CHUNK:lo + ROW_CHUNK + halo, :]
    prev = jnp.where(starts, jnp.zeros_like(prev), prev)
    nxt = jnp.where(ends, jnp.zeros_like(nxt), nxt)
    return prev, cur_ref[lo:lo + ROW_CHUNK, :], nxt


CONF_HALO = 16
CONF_ROW_CHUNK = 64
SC_HALO = 8


def _conformer_core(up_ref, uc_ref, un_ref, dw_ref, dwb_ref, lng_ref, lnb_ref, scr, cv, *,
                    c, n_chunks, t0, t_lat, seq, ctx_len):
    n_lg, taps = dw_ref.shape[0], dw_ref.shape[1]
    prev, cur, nxt = _chunk_with_halo(up_ref, uc_ref, un_ref, c, n_chunks, CONF_HALO, t0, t_lat,
                                      seq, ctx_len)
    for g in range(n_lg):
        cols = slice(g * LANES, (g + 1) * LANES)
        scr[c, g, 0:CONF_HALO, :] = prev[:, cols]
        scr[c, g, CONF_HALO:CONF_HALO + ROW_CHUNK, :] = cur[:, cols]
        scr[c, g, CONF_HALO + ROW_CHUNK:, :] = nxt[:, cols]
    base = CONF_HALO - (taps - 1) // 2
    for g in range(n_lg):
        w_g = dw_ref[g]
        b_g = dwb_ref[g]
        for r0 in range(0, ROW_CHUNK, CONF_ROW_CHUNK):
            acc = jnp.broadcast_to(b_g, (CONF_ROW_CHUNK, LANES))
            for k in range(taps):
                acc = acc + w_g[k:k + 1, :] * scr[c, g, r0 + base + k:r0 + base + k + CONF_ROW_CHUNK, :]
            cv[c, g, r0:r0 + CONF_ROW_CHUNK, :] = acc
    u = jnp.concatenate([cv[c, g] for g in range(n_lg)], axis=-1)
    mu = jnp.mean(u, axis=-1, keepdims=True)
    uc = u - mu
    var = jnp.mean(uc * uc, axis=-1, keepdims=True)
    y = uc * lax.rsqrt(var + EPS) * lng_ref[...] + lnb_ref[...]
    return (y * jax.nn.sigmoid(y)).astype(BF16)


def _shortconv_core(gp_ref, gc_ref, gn_ref, bg_ref, cw_ref, scr, *, c, n_chunks, t0, t_lat, seq,
                    ctx_len):
    prev, cur, nxt = _chunk_with_halo(gp_ref, gc_ref, gn_ref, c, n_chunks, SC_HALO, t0, t_lat,
                                      seq, ctx_len)
    scr[c, 0:SC_HALO, :] = prev
    scr[c, SC_HALO:SC_HALO + ROW_CHUNK, :] = cur
    scr[c, SC_HALO + ROW_CHUNK:, :] = nxt
    conv = (cw_ref[0:1, :] * scr[c, SC_HALO - 1:SC_HALO - 1 + ROW_CHUNK, :]
            + cw_ref[1:2, :] * scr[c, SC_HALO:SC_HALO + ROW_CHUNK, :]
            + cw_ref[2:3, :] * scr[c, SC_HALO + 1:SC_HALO + 1 + ROW_CHUNK, :])
    lo = c * ROW_CHUNK
    return (bg_ref[lo:lo + ROW_CHUNK, :] * conv).astype(BF16)


def _slab_rows(a_refs, j, rows, n_lat_tiles):
    if len(a_refs) == 1:
        return a_refs[0][j, rows, :]
    return jnp.where(pl.program_id(0) < n_lat_tiles, a_refs[0][j, rows, :], a_refs[1][j, rows, :])


def _post_mixer_kernel(*refs, variant, n_in, n_x, n_lat_tiles, a_lat_tiles, n_scr, has_bias, final,
                       tm, t_lat, seq, ctx_len):
    mixer_refs = refs[:n_in]
    x_refs = refs[n_in:n_in + n_x]
    mod_ref, g2_ref, wout_ref, w1_ref, w2_ref = refs[n_in + n_x:n_in + n_x + 5]
    rest = list(refs[n_in + n_x + 5:])
    bout_ref = rest.pop(0) if has_bias else None
    fg_ref = rest.pop(0) if final else None
    o_ref = rest.pop(0)
    scr_refs = rest
    assert len(scr_refs) == n_scr

    def tail(a, rows):
        y = _dot(a, wout_ref[...])
        if has_bias:
            y = y + bout_ref[...]
        x1 = _tok_rows(x_refs, rows, n_lat_tiles) + mod_ref[2:3, :] * y
        m = _modulate(_rmsnorm(x1, g2_ref[...]), mod_ref, 3).astype(BF16)
        h = jnp.maximum(_dot(m, w1_ref[...]), 0.0)
        h = (h * h).astype(BF16)
        x2 = x1 + mod_ref[5:6, :] * _dot(h, w2_ref[...])
        if final:
            x2 = _rmsnorm(x2, fg_ref[...])
        o_ref[rows, :] = x2

    n_chunks = tm // ROW_CHUNK
    geom = dict(n_chunks=n_chunks, t0=pl.program_id(0) * tm, t_lat=t_lat, seq=seq, ctx_len=ctx_len)
    for c in range(n_chunks):
        rows = slice(c * ROW_CHUNK, (c + 1) * ROW_CHUNK)
        if variant == "plain":
            a = jnp.concatenate([_slab_rows(mixer_refs, j, rows, a_lat_tiles)
                                 for j in range(mixer_refs[0].shape[0])], axis=1)
        else:
            core = _conformer_core if variant == "conformer" else _shortconv_core
            a = core(*mixer_refs, *scr_refs, c=c, **geom)
        tail(a, rows)


def _post_mixer(variant, mixer_args, mixer_specs, scratch, x, mod, g2, wout, bout, w1, w2,
                final_g, *, layer, n_rows, mod_row, tm, t_lat, seq, ctx_len):
    d, f = w1.shape[1:]
    x_specs, x_args, n_lat_tiles = _tok_specs(x, tm)
    in_specs = list(mixer_specs) + x_specs + [
        pl.BlockSpec((None, None, N_MOD, d), lambda i: (layer, mod_row(i), 0, 0)),
        pl.BlockSpec((None, 1, d), lambda i: (layer, 0, 0)),
        _resident((d, d)),
        _resident((d, f), layer),
        _resident((f, d), layer),
    ]
    args = list(mixer_args) + x_args + [mod, g2, wout, w1, w2]
    if bout is not None:
        in_specs.append(_resident((1, d)))
        args.append(bout)
    if final_g is not None:
        in_specs.append(_resident((1, d)))
        args.append(final_g)
    kern = functools.partial(_post_mixer_kernel, variant=variant, n_in=len(mixer_args),
                             n_x=len(x_args), n_lat_tiles=n_lat_tiles,
                             a_lat_tiles=t_lat // tm, n_scr=len(scratch),
                             has_bias=bout is not None,
                             final=final_g is not None, tm=tm, t_lat=t_lat, seq=seq,
                             ctx_len=ctx_len)
    return pl.pallas_call(
        kern,
        grid=(n_rows // tm,),
        in_specs=in_specs,
        out_specs=pl.BlockSpec((tm, d), lambda i: (i, 0)),
        out_shape=jax.ShapeDtypeStruct((n_rows, d), F32),
        scratch_shapes=list(scratch),
        compiler_params=_cparams(("parallel",)),
        name="post_mixer_" + variant,
    )(*args)


def _halo_specs(n_total_rows, tm, halo, d):
    hb = tm // halo
    n_hblk = n_total_rows // halo
    return [
        pl.BlockSpec((halo, d), lambda i: (jnp.maximum(i * hb - 1, 0), 0)),
        pl.BlockSpec((tm, d), lambda i: (i, 0)),
        pl.BlockSpec((halo, d), lambda i: (jnp.minimum((i + 1) * hb, n_hblk - 1), 0)),
    ]


def kernel(x, c, ctx, c_ctx, mod_w, mod_b, norm1_g, norm2_g, mlp_w1, mlp_w2, na_wqkv, na_wo,
           na_rpb, cv_w1, cv_b1, cv_dw, cv_dwb, cv_ln_g, cv_ln_b, cv_w2, cv_b2, sc_win, sc_conv,
           sc_wout, final_g):
    batch, seq, d = x.shape
    ctx_len = ctx.shape[1]
    depth = mod_w.shape[0]
    t_lat = batch * seq
    t_all = t_lat + batch * ctx_len
    head_dim = d // N_HEADS
    assert 2 * head_dim == LANES and batch < MOD_ROWS and seq % GRID_W == 0
    rows = seq // GRID_W
    tm = 2 * ROW_CHUNK
    tm_in = 4 * ROW_CHUNK
    n_chunks = tm // ROW_CHUNK
    assert seq % tm_in == 0 and (batch * ctx_len) % tm_in == 0 and tm_in % tm == 0
    assert seq % ROW_CHUNK == 0 and ctx_len % ROW_CHUNK == 0
    last_attn = max(i for i in range(depth) if i % N_MIXERS == 0)
    n_lg = d // LANES

    def mod_row_fn(tile):
        def f(i):
            return jnp.where(i * tile < t_lat, (i * tile) // seq, batch)
        return f

    cc = jnp.concatenate([c, c_ctx[None], jnp.zeros((MOD_ROWS - batch - 1, d), F32)], axis=0)
    mod = _modulation(cc, mod_w, mod_b)

    tok = (x.reshape(t_lat, d), ctx.reshape(batch * ctx_len, d))
    g1 = norm1_g.reshape(depth, 1, d)
    g2 = norm2_g.reshape(depth, 1, d)
    geo = _attn_geometry(rows)
    w1_all = mlp_w1.astype(BF16)
    w2_all = mlp_w2.astype(BF16)

    for i in range(depth):
        kind = i % N_MIXERS
        slot = i // N_MIXERS
        ctx_live = i < last_attn
        n_rows = t_all if ctx_live else t_lat
        inp = functools.partial(_in_proj, tok, mod, g1, layer=i, mod_row=mod_row_fn(tm_in),
                                tm=tm_in)
        post = functools.partial(
            _post_mixer, x=tok, mod=mod, g2=g2, w1=w1_all, w2=w2_all,
            final_g=final_g[None] if i == depth - 1 else None,
            layer=i, n_rows=n_rows, t_lat=t_lat, seq=seq, ctx_len=ctx_len)
        if kind == 0:
            (qkv,) = inp(na_wqkv[slot].astype(BF16), None, n_rows=t_all,
                         write_out=_write_qkv(d, head_dim ** -0.5 * LOG2E), name="in_proj_qkv",
                         **_lane_group_tiles(t_all, 3 * n_lg, BF16, tm_in))
            bias_tab = _attn_bias_table(na_rpb[slot], geo)
            o = [_attention_latent(qkv, bias_tab, geo, batch=batch, seq=seq, ctx_len=ctx_len)]
            o_specs = [pl.BlockSpec((n_lg, tm, LANES), lambda r: (0, r, 0))]
            if ctx_live:
                o.append(_attention_ctx(qkv, batch=batch, seq=seq, ctx_len=ctx_len))
                n_lat = t_lat // tm
                o_specs = [
                    pl.BlockSpec((n_lg, tm, LANES), lambda r: (0, jnp.minimum(r, n_lat - 1), 0)),
                    pl.BlockSpec((n_lg, tm, LANES), lambda r: (0, jnp.maximum(r - n_lat, 0), 0))]
            tok = post("plain", o, o_specs, [], wout=na_wo[slot].astype(BF16), bout=None,
                       mod_row=mod_row_fn(tm), tm=tm)
        elif kind == 1:
            (u,) = inp(cv_w1[slot].astype(BF16), cv_b1[slot][None], n_rows=n_rows,
                       write_out=_write_glu(d), name="in_proj_glu",
                       **_row_tiles(n_rows, [d], [F32], tm_in))
            taps = cv_dw.shape[1]
            dw = cv_dw[slot].reshape(taps, n_lg, LANES).transpose(1, 0, 2)
            args = [u, u, u, dw, cv_dwb[slot].reshape(n_lg, 1, LANES), cv_ln_g[slot][None],
                    cv_ln_b[slot][None]]
            specs = _halo_specs(n_rows, tm, CONF_HALO, d) + [
                _resident((n_lg, taps, LANES)), _resident((n_lg, 1, LANES)),
                _resident((1, d)), _resident((1, d))]
            scratch = [pltpu.VMEM((n_chunks, n_lg, ROW_CHUNK + 2 * CONF_HALO, LANES), F32),
                       pltpu.VMEM((n_chunks, n_lg, ROW_CHUNK, LANES), F32)]
            tok = post("conformer", args, specs, scratch, wout=cv_w2[slot].astype(BF16),
                       bout=cv_b2[slot][None], mod_row=mod_row_fn(tm), tm=tm)
        else:
            bg, g = inp(sc_win[slot].astype(BF16), None, n_rows=n_rows,
                        write_out=_write_gated(d), name="in_proj_gated",
                        **_row_tiles(n_rows, [d, d], [F32, F32], tm_in))
            args = [g, g, g, bg, sc_conv[slot]]
            specs = _halo_specs(n_rows, tm, SC_HALO, d) + [
                pl.BlockSpec((tm, d), lambda r: (r, 0)), _resident(sc_conv[slot].shape)]
            scratch = [pltpu.VMEM((n_chunks, ROW_CHUNK + 2 * SC_HALO, d), F32)]
            tok = post("shortconv", args, specs, scratch, wout=sc_wout[slot].astype(BF16),
                       bout=None, mod_row=mod_row_fn(tm), tm=tm)
    return tok[:t_lat].reshape(batch, seq, d)
```
